```python
import math
import jax, jax.numpy as jnp
from jax import lax
import numpy as np

D_MODEL = 1024
BATCH = 16
SEQ = 2048
DEPTH = 1

HEAD_DIM = 64
SWA_Q_HEADS = 8
SWA_KV_HEADS = 2
SB_HEADS = 8
WINDOW = 128
BLOCK = 128
ROPE_THETA = 10000.0
EPS = 1e-6
SWA_WIDTH = SWA_Q_HEADS * HEAD_DIM
SWA_KV_WIDTH = SWA_KV_HEADS * HEAD_DIM
SB_WIDTH = SB_HEADS * HEAD_DIM
MIX_WIDTH = SWA_WIDTH + SB_WIDTH
SPLITS = (SWA_WIDTH, SWA_KV_WIDTH, SWA_KV_WIDTH, SWA_WIDTH, SB_WIDTH, SB_WIDTH, SB_WIDTH, SB_WIDTH)
IN_WIDTH = sum(SPLITS)

kernel_name = "hybrid_swa_sink_stickbreaking_heads"


def rmsnorm(x, g):
    xf = x.astype(jnp.float32)
    y = xf * lax.rsqrt(jnp.mean(xf * xf, axis=-1, keepdims=True) + EPS)
    return (y * g.astype(jnp.float32)).astype(x.dtype)


def rope(x, positions):
    half = x.shape[-1] // 2
    inv_freq = ROPE_THETA ** (-jnp.arange(half, dtype=jnp.float32) * 2.0 / x.shape[-1])
    ang = positions.astype(jnp.float32)[..., None] * inv_freq
    cos = jnp.cos(ang)[:, :, None, :]
    sin = jnp.sin(ang)[:, :, None, :]
    xf = x.astype(jnp.float32)
    x1, x2 = xf[..., :half], xf[..., half:]
    out = jnp.concatenate([x1 * cos - x2 * sin, x2 * cos + x1 * sin], axis=-1)
    return out.astype(x.dtype)


def swa_sink_attention(q, k, v, sinks):
    B, S, Hq, d = q.shape
    Hkv = k.shape[2]
    G = Hq // Hkv
    nb = S // BLOCK
    scale = 1.0 / math.sqrt(d)
    qb = q.reshape(B, nb, BLOCK, Hkv, G, d)
    kb = k.reshape(B, nb, BLOCK, Hkv, d)
    vb = v.reshape(B, nb, BLOCK, Hkv, d)
    kprev = jnp.concatenate([jnp.zeros_like(kb[:, :1]), kb[:, :-1]], axis=1)
    vprev = jnp.concatenate([jnp.zeros_like(vb[:, :1]), vb[:, :-1]], axis=1)
    kw = jnp.concatenate([kprev, kb], axis=2)
    vw = jnp.concatenate([vprev, vb], axis=2)
    scores = jnp.einsum('bnqhgd,bnkhd->bnhgqk', qb, kw).astype(jnp.float32) * scale
    q_idx = jnp.arange(nb)[:, None] * BLOCK + jnp.arange(BLOCK)[None, :]
    k_idx = jnp.arange(nb)[:, None] * BLOCK - BLOCK + jnp.arange(2 * BLOCK)[None, :]
    diff = q_idx[:, :, None] - k_idx[:, None, :]
    mask = (diff >= 0) & (diff < WINDOW) & (k_idx[:, None, :] >= 0)
    scores = jnp.where(mask[None, :, None, None], scores, -jnp.inf)
    sink = sinks.astype(jnp.float32).reshape(Hkv, G)[None, None, :, :, None, None]
    m = jnp.maximum(jnp.max(scores, axis=-1, keepdims=True), sink)
    p = jnp.exp(scores - m)
    probs = p / (jnp.sum(p, axis=-1, keepdims=True) + jnp.exp(sink - m))
    out = jnp.einsum('bnhgqk,bnkhd->bnqhgd', probs.astype(v.dtype), vw)
    return out.reshape(B, S, Hq * d)


def stick_breaking_attention(q, k, v):
    B, S, H, d = q.shape
    nb = S // BLOCK
    scale = 1.0 / math.sqrt(d)
    outs = []
    for i in range(nb):
        t0 = i * BLOCK
        L = t0 + BLOCK
        qi = q[:, t0:L]
        kp = k[:, :L]
        vp = v[:, :L]
        z = jnp.einsum('bqhd,bkhd->bhqk', qi, kp).astype(jnp.float32) * scale
        t_idx = t0 + jnp.arange(BLOCK)[:, None]
        s_idx = jnp.arange(L)[None, :]
        causal = s_idx < t_idx
        log_fail = jnp.where(causal, jax.nn.log_sigmoid(-z), 0.0)
        after = lax.cumsum(log_fail, axis=3, reverse=True) - log_fail
        w = jnp.where(causal, jnp.exp(jax.nn.log_sigmoid(z) + after), 0.0)
        outs.append(jnp.einsum('bhqk,bkhd->bqhd', w.astype(v.dtype), vp))
    out = jnp.concatenate(outs, axis=1)
    return out.reshape(B, S, H * d)


def setup_inputs(seed: int = 0) -> dict:
    key = jax.random.key(seed)
    ks = jax.random.split(key, 8)
    x = jax.random.normal(ks[0], (BATCH, SEQ, D_MODEL), jnp.float32)
    positions = jnp.broadcast_to(jnp.arange(SEQ, dtype=jnp.int32)[None, :], (BATCH, SEQ)).astype(jnp.int32)
    norm_gain = 1.0 + 0.02 * jax.random.normal(ks[1], (DEPTH, D_MODEL), jnp.float32)
    w_in = jax.random.normal(ks[2], (DEPTH, D_MODEL, IN_WIDTH), jnp.float32) * D_MODEL ** -0.5
    q_norm_gain = 1.0 + 0.02 * jax.random.normal(ks[3], (DEPTH, HEAD_DIM), jnp.float32)
    k_norm_gain = 1.0 + 0.02 * jax.random.normal(ks[4], (DEPTH, HEAD_DIM), jnp.float32)
    sinks = 0.5 * jax.random.normal(ks[5], (DEPTH, SWA_Q_HEADS), jnp.float32)
    w_out = jax.random.normal(ks[6], (DEPTH, MIX_WIDTH, D_MODEL), jnp.float32) * MIX_WIDTH ** -0.5
    return {"x": x, "positions": positions, "norm_gain": norm_gain, "w_in": w_in,
            "q_norm_gain": q_norm_gain, "k_norm_gain": k_norm_gain, "sinks": sinks, "w_out": w_out}


def reference(x, positions, norm_gain, w_in, q_norm_gain, k_norm_gain, sinks, w_out):
    B, S, _ = x.shape
    split_points = list(np.cumsum(SPLITS)[:-1])
    for l in range(DEPTH):
        h = rmsnorm(x, norm_gain[l])
        proj = jnp.einsum('bsd,de->bse', h, w_in[l])
        qa, ka, va, ga, qb, kb, vb, gb = jnp.split(proj, split_points, axis=-1)
        qa = rope(rmsnorm(qa.reshape(B, S, SWA_Q_HEADS, HEAD_DIM), q_norm_gain[l]), positions)
        ka = rope(rmsnorm(ka.reshape(B, S, SWA_KV_HEADS, HEAD_DIM), k_norm_gain[l]), positions)
        va = va.reshape(B, S, SWA_KV_HEADS, HEAD_DIM)
        ya = swa_sink_attention(qa, ka, va, sinks[l]) * jax.nn.silu(ga)
        qb = qb.reshape(B, S, SB_HEADS, HEAD_DIM)
        kb = kb.reshape(B, S, SB_HEADS, HEAD_DIM)
        vb = vb.reshape(B, S, SB_HEADS, HEAD_DIM)
        yb = stick_breaking_attention(qb, kb, vb) * jax.nn.silu(gb)
        y = jnp.concatenate([ya, yb], axis=-1)
        x = x + jnp.einsum('bse,ed->bsd', y, w_out[l])
    return x
```

```python
import functools
import math

import jax
import jax.numpy as jnp
from jax import lax
from jax.experimental import pallas as pl
from jax.experimental.pallas import tpu as pltpu

HEAD_DIM = 64
SWA_Q_HEADS = 8
SWA_KV_HEADS = 2
SB_HEADS = 8
BLOCK = 128
ROPE_THETA = 10000.0
EPS = 1e-6
LANES = 128
CHUNK = 256

SWA_WIDTH = SWA_Q_HEADS * HEAD_DIM
SWA_KV_WIDTH = SWA_KV_HEADS * HEAD_DIM
SB_WIDTH = SB_HEADS * HEAD_DIM
MIX_WIDTH = SWA_WIDTH + SB_WIDTH

SRC_QA = 0
SRC_KA = SRC_QA + SWA_WIDTH
SRC_VA = SRC_KA + SWA_KV_WIDTH
SRC_GA = SRC_VA + SWA_KV_WIDTH
SRC_QB = SRC_GA + SWA_WIDTH
SRC_KB = SRC_QB + SB_WIDTH
SRC_VB = SRC_KB + SB_WIDTH
SRC_GB = SRC_VB + SB_WIDTH

SLOT = 512
SLOT_QA, SLOT_GA, SLOT_QB, SLOT_KB, SLOT_VB, SLOT_GB, SLOT_KVA = range(7)
ACT_WIDTH = 7 * SLOT

VMEM_LIMIT_BYTES = 56 * 1024 * 1024

_NT = (((1,), (1,)), ((), ()))


def _lane_iota(shape):
    return lax.broadcasted_iota(jnp.int32, shape, len(shape) - 1)


def _split_bf16(v):
    hi = v.astype(jnp.bfloat16)
    lo = (v - hi.astype(jnp.float32)).astype(jnp.bfloat16)
    return hi, lo


def _dot_split(v, rhs):
    hi, lo = _split_bf16(v)
    return jnp.dot(hi, rhs, preferred_element_type=jnp.float32) + jnp.dot(
        lo, rhs, preferred_element_type=jnp.float32
    )


def _silu(g):
    return g * (1.0 / (1.0 + jnp.exp(-g)))


def _inproj_kernel(x_ref, pos_ref, gain_ref, w_ref, qgain_ref, kgain_ref, invf_ref, o_ref):
    x = x_ref[...]
    ms = jnp.mean(x * x, axis=-1, keepdims=True)
    h = (x * lax.rsqrt(ms + EPS) * gain_ref[...]).astype(jnp.bfloat16)

    lane = _lane_iota((1, LANES))
    first_half = (lane % HEAD_DIM) < (HEAD_DIM // 2)
    ang = pos_ref[...].astype(jnp.float32) * invf_ref[...]
    cos = jnp.cos(ang)
    sin = jnp.sin(ang)
    sin_signed = jnp.where(first_half, -sin, sin)

    r = lax.broadcasted_iota(jnp.int32, (LANES, LANES), 0) // HEAD_DIM
    c = lax.broadcasted_iota(jnp.int32, (LANES, LANES), 1) // HEAD_DIM
    head_ones = jnp.where(r == c, 1.0, 0.0).astype(jnp.bfloat16)

    def norm_rope(a, head_gain, scale):
        ss = _dot_split(a * a, head_ones)
        y = a * lax.rsqrt(ss * (1.0 / HEAD_DIM) + EPS) * head_gain
        partner = jnp.where(
            first_half, pltpu.roll(y, LANES - HEAD_DIM // 2, 1), pltpu.roll(y, HEAD_DIM // 2, 1)
        )
        y = y * cos + partner * sin_signed
        return y * scale if scale != 1.0 else y

    def project(src_col):
        return jnp.dot(h, w_ref[:, src_col : src_col + CHUNK], preferred_element_type=jnp.float32)

    def store(slot, col, val):
        o_ref[:, slot * SLOT + col : slot * SLOT + col + val.shape[1]] = val.astype(o_ref.dtype)

    q_scale = 1.0 / math.sqrt(HEAD_DIM)
    for j in range(SLOT // CHUNK):
        col = j * CHUNK
        acc = project(SRC_QA + col)
        for half in range(CHUNK // LANES):
            a = acc[:, half * LANES : (half + 1) * LANES]
            store(SLOT_QA, col + half * LANES, norm_rope(a, qgain_ref[...], q_scale))
        store(SLOT_GA, col, _silu(project(SRC_GA + col)))
        store(SLOT_QB, col, project(SRC_QB + col) * q_scale)
        store(SLOT_KB, col, project(SRC_KB + col))
        store(SLOT_VB, col, project(SRC_VB + col))
        store(SLOT_GB, col, _silu(project(SRC_GB + col)))

    acc = project(SRC_KA)
    k = norm_rope(acc[:, :LANES], kgain_ref[...], 1.0)
    v = acc[:, LANES:]
    low = lane < HEAD_DIM
    for base, t in ((0, k), (2 * LANES, v)):
        swapped = pltpu.roll(t, HEAD_DIM, 1)
        store(SLOT_KVA, base, jnp.where(low, t, swapped))
        store(SLOT_KVA, base + LANES, jnp.where(low, swapped, t))


def _inproj(x2, pos2, gain, w_bf16, qgain, kgain, invf, tm):
    n, d = x2.shape
    full = lambda i: (0, 0)
    return pl.pallas_call(
        _inproj_kernel,
        out_shape=jax.ShapeDtypeStruct((n, ACT_WIDTH), jnp.bfloat16),
        grid=(n // tm,),
        in_specs=[
            pl.BlockSpec((tm, d), lambda i: (i, 0)),
            pl.BlockSpec((tm, 1), lambda i: (i, 0)),
            pl.BlockSpec((1, d), full),
            pl.BlockSpec(w_bf16.shape, full),
            pl.BlockSpec((1, LANES), full),
            pl.BlockSpec((1, LANES), full),
            pl.BlockSpec((1, LANES), full),
        ],
        out_specs=pl.BlockSpec((tm, ACT_WIDTH), lambda i: (i, 0)),
        compiler_params=pltpu.CompilerParams(
            dimension_semantics=("arbitrary",), vmem_limit_bytes=VMEM_LIMIT_BYTES
        ),
        name="inproj",
    )(x2, pos2, gain, w_bf16, qgain, kgain, invf)


def _mixer_kernel(
    sinks_ref, x_ref, qa_ref, ga_ref, qb_ref, gb_ref, kb_ref, vb_ref, kva_ref, wout_ref,
    o_ref, y_ref, acc_ref, fail_ref, *, tq,
):
    step = pl.program_id(1)
    row2 = lax.broadcasted_iota(jnp.int32, (2 * BLOCK, LANES), 0)
    lane2 = lax.broadcasted_iota(jnp.int32, (2 * BLOCK, LANES), 1)
    qrow = row2 & (BLOCK - 1)
    own_head = (row2 < BLOCK) == (lane2 < HEAD_DIM)
    in_cur_block = lane2 <= qrow
    strictly_causal = lane2 < qrow
    low = _lane_iota((1, LANES)) < HEAD_DIM
    kr = lax.broadcasted_iota(jnp.int32, (BLOCK, 2 * BLOCK), 0)
    kc = lax.broadcasted_iota(jnp.int32, (BLOCK, 2 * BLOCK), 1)
    suffix_total = jnp.where(jnp.logical_or(kr > kc, kc >= BLOCK), 1.0, 0.0).astype(jnp.bfloat16)

    def stack_heads(q_pair):
        q2 = jnp.concatenate([q_pair, q_pair], axis=0)
        return jnp.where(own_head, q2, jnp.zeros_like(q2))

    def unstack_heads(o2):
        return jnp.where(low, o2[:BLOCK], o2[BLOCK:])

    def sub_block(sb, carry_unused):
        blk = step * (tq // BLOCK) + sb
        r0 = pl.multiple_of(sb * BLOCK, BLOCK)
        k0 = pl.multiple_of(blk * BLOCK, BLOCK)
        kp = pl.multiple_of(jnp.maximum(blk - 1, 0) * BLOCK, BLOCK)

        valid = jnp.logical_or(in_cur_block, (jnp.zeros_like(lane2) + blk) > 0)
        for p in range(SWA_Q_HEADS // 2):
            g = (2 * p) // (SWA_Q_HEADS // SWA_KV_HEADS)
            q2 = stack_heads(qa_ref[0, pl.ds(r0, BLOCK), p * LANES : (p + 1) * LANES])
            k_cols = slice(g * LANES, (g + 1) * LANES)
            v_cols = slice((SWA_KV_HEADS + g) * LANES, (SWA_KV_HEADS + g + 1) * LANES)
            s_cur = lax.dot_general(
                q2, kva_ref[0, pl.ds(k0, BLOCK), k_cols], _NT, preferred_element_type=jnp.float32
            )
            s_prev = lax.dot_general(
                q2, kva_ref[0, pl.ds(kp, BLOCK), k_cols], _NT, preferred_element_type=jnp.float32
            )
            s = jnp.where(valid, jnp.where(in_cur_block, s_cur, s_prev), -jnp.inf)
            sink = jnp.where(row2[:, :1] < BLOCK, sinks_ref[0, 2 * p], sinks_ref[0, 2 * p + 1])
            m = jnp.maximum(jnp.max(s, axis=-1, keepdims=True), sink)
            e = jnp.exp(s - m)
            denom = jnp.sum(e, axis=-1, keepdims=True) + jnp.exp(sink - m)
            e_cur = jnp.where(in_cur_block, e, 0.0).astype(jnp.bfloat16)
            e_prev = jnp.where(in_cur_block, 0.0, e).astype(jnp.bfloat16)
            o2 = jnp.dot(
                e_cur, kva_ref[0, pl.ds(k0, BLOCK), v_cols], preferred_element_type=jnp.float32
            ) + jnp.dot(e_prev, kva_ref[0, pl.ds(kp, BLOCK), v_cols], preferred_element_type=jnp.float32)
            o2 = o2 / denom
            gate = ga_ref[0, pl.ds(r0, BLOCK), p * LANES : (p + 1) * LANES].astype(jnp.float32)
            y_ref[pl.ds(r0, BLOCK), p * LANES : (p + 1) * LANES] = (unstack_heads(o2) * gate).astype(
                y_ref.dtype
            )

        for p in range(SB_HEADS // 2):
            cols = slice(p * LANES, (p + 1) * LANES)
            q2 = stack_heads(qb_ref[0, pl.ds(r0, BLOCK), cols])

            def tile(ks, diagonal):
                k_t = kb_ref[0, pl.ds(ks, BLOCK), cols]
                v_t = vb_ref[0, pl.ds(ks, BLOCK), cols]
                z = lax.dot_general(q2, k_t, _NT, preferred_element_type=jnp.float32)
                sp = jnp.maximum(z, 0.0) + jnp.log(1.0 + jnp.exp(-jnp.abs(z)))
                if diagonal:
                    sp = jnp.where(strictly_causal, sp, 0.0)
                sums = _dot_split(sp, suffix_total)
                later, total = sums[:, :BLOCK], sums[:, BLOCK:]
                w = jnp.exp(z - sp - later)
                if diagonal:
                    w = jnp.where(strictly_causal, w, 0.0)
                pv = jnp.dot(w.astype(jnp.bfloat16), v_t, preferred_element_type=jnp.float32)
                if diagonal:
                    acc_ref[...] = pv
                    fail_ref[...] = total
                else:
                    fail = fail_ref[...]
                    acc_ref[...] += jnp.exp(-fail) * pv
                    fail_ref[...] = fail + total

            tile(k0, True)

            def body(n, c):
                tile(pl.multiple_of((blk - n) * BLOCK, BLOCK), False)
                return c

            lax.fori_loop(1, blk + 1, body, 0)
            gate = gb_ref[0, pl.ds(r0, BLOCK), cols].astype(jnp.float32)
            y_ref[pl.ds(r0, BLOCK), SWA_WIDTH + p * LANES : SWA_WIDTH + (p + 1) * LANES] = (
                unstack_heads(acc_ref[...]) * gate
            ).astype(y_ref.dtype)
        return carry_unused

    lax.fori_loop(0, tq // BLOCK, sub_block, 0)
    o_ref[0] = x_ref[0] + jnp.dot(y_ref[...], wout_ref[...], preferred_element_type=jnp.float32)


def _mixer(sinks, x, act, wout_bf16, tq):
    b, s, d = x.shape
    blk = lambda slot: pl.BlockSpec((1, tq, SLOT), lambda bi, i, slot=slot: (bi, i, slot))
    seq = lambda slot: pl.BlockSpec((1, s, SLOT), lambda bi, i, slot=slot: (bi, 0, slot))
    return pl.pallas_call(
        functools.partial(_mixer_kernel, tq=tq),
        out_shape=jax.ShapeDtypeStruct((b, s, d), jnp.float32),
        grid=(b, s // tq),
        in_specs=[
            pl.BlockSpec(memory_space=pltpu.SMEM),
            pl.BlockSpec((1, tq, d), lambda bi, i: (bi, i, 0)),
            blk(SLOT_QA), blk(SLOT_GA), blk(SLOT_QB), blk(SLOT_GB),
            seq(SLOT_KB), seq(SLOT_VB), seq(SLOT_KVA),
            pl.BlockSpec(wout_bf16.shape, lambda bi, i: (0, 0)),
        ],
        out_specs=pl.BlockSpec((1, tq, d), lambda bi, i: (bi, i, 0)),
        scratch_shapes=[
            pltpu.VMEM((tq, MIX_WIDTH), jnp.bfloat16),
            pltpu.VMEM((2 * BLOCK, LANES), jnp.float32),
            pltpu.VMEM((2 * BLOCK, LANES), jnp.float32),
        ],
        compiler_params=pltpu.CompilerParams(
            dimension_semantics=("arbitrary", "arbitrary"), vmem_limit_bytes=VMEM_LIMIT_BYTES
        ),
        name="mixer",
    )(sinks, x, act, act, act, act, act, act, act, wout_bf16)


def kernel(x, positions, norm_gain, w_in, q_norm_gain, k_norm_gain, sinks, w_out):
    b, s, d = x.shape
    depth = w_in.shape[0]
    half = HEAD_DIM // 2
    inv_freq = ROPE_THETA ** (-jnp.arange(half, dtype=jnp.float32) * 2.0 / HEAD_DIM)
    invf = jnp.tile(inv_freq, LANES // half).reshape(1, LANES)
    pos2 = positions.reshape(b * s, 1)
    for l in range(depth):
        act = _inproj(
            x.reshape(b * s, d),
            pos2,
            norm_gain[l].reshape(1, d),
            w_in[l].astype(jnp.bfloat16),
            jnp.tile(q_norm_gain[l], LANES // HEAD_DIM).reshape(1, LANES),
            jnp.tile(k_norm_gain[l], LANES // HEAD_DIM).reshape(1, LANES),
            invf,
            tm=512,
        )
        x = _mixer(
            sinks[l].reshape(1, SWA_Q_HEADS),
            x,
            act.reshape(b, s, ACT_WIDTH),
            w_out[l].astype(jnp.bfloat16),
            tq=512,
        )
    return x
```

```python
import functools
import math

import jax
import jax.numpy as jnp
from jax import lax
from jax.experimental import pallas as pl
from jax.experimental.pallas import tpu as pltpu

HEAD_DIM = 64
SWA_Q_HEADS = 8
SWA_KV_HEADS = 2
SB_HEADS = 8
BLOCK = 128
ROPE_THETA = 10000.0
EPS = 1e-6
LANES = 128
CHUNK = 256
LOG2E = math.log2(math.e)

SWA_WIDTH = SWA_Q_HEADS * HEAD_DIM
SWA_KV_WIDTH = SWA_KV_HEADS * HEAD_DIM
SB_WIDTH = SB_HEADS * HEAD_DIM
MIX_WIDTH = SWA_WIDTH + SB_WIDTH
SB_PAIRS = SB_HEADS // 2

SRC_QA = 0
SRC_KA = SRC_QA + SWA_WIDTH
SRC_VA = SRC_KA + SWA_KV_WIDTH
SRC_GA = SRC_VA + SWA_KV_WIDTH
SRC_QB = SRC_GA + SWA_WIDTH
SRC_KB = SRC_QB + SB_WIDTH
SRC_VB = SRC_KB + SB_WIDTH
SRC_GB = SRC_VB + SB_WIDTH

SLOT = 512
SLOT_QA, SLOT_GA, SLOT_QB, SLOT_KB, SLOT_VB, SLOT_GB, SLOT_KVA = range(7)
ACT_WIDTH = 7 * SLOT

VMEM_LIMIT_BYTES = 56 * 1024 * 1024

SB_DONE_LOG2 = 127.0
MASKED_SCORE = -1e30

_NT = (((1,), (1,)), ((), ()))


def _lane_iota(shape):
    return lax.broadcasted_iota(jnp.int32, shape, len(shape) - 1)


def _split_bf16(v):
    hi = v.astype(jnp.bfloat16)
    lo = (v - hi.astype(jnp.float32)).astype(jnp.bfloat16)
    return hi, lo


def _dot_split(v, rhs):
    hi, lo = _split_bf16(v)
    return jnp.dot(hi, rhs, preferred_element_type=jnp.float32) + jnp.dot(
        lo, rhs, preferred_element_type=jnp.float32
    )


def _silu(g):
    return g * (1.0 / (1.0 + jnp.exp(-g)))


def _softplus_log2(z2):
    return jnp.maximum(z2, 0.0) + jnp.log2(1.0 + jnp.exp2(-jnp.abs(z2)))


def _inproj_kernel(x_ref, pos_ref, gain_ref, w_ref, qgain_ref, kgain_ref, invf_ref, o_ref):
    x = x_ref[...]
    ms = jnp.mean(x * x, axis=-1, keepdims=True)
    h = (x * lax.rsqrt(ms + EPS) * gain_ref[...]).astype(jnp.bfloat16)

    lane = _lane_iota((1, LANES))
    first_half = (lane % HEAD_DIM) < (HEAD_DIM // 2)
    ang = pos_ref[...].astype(jnp.float32) * invf_ref[...]
    cos = jnp.cos(ang)
    sin = jnp.sin(ang)
    sin_signed = jnp.where(first_half, -sin, sin)

    r = lax.broadcasted_iota(jnp.int32, (LANES, LANES), 0) // HEAD_DIM
    c = lax.broadcasted_iota(jnp.int32, (LANES, LANES), 1) // HEAD_DIM
    head_ones = jnp.where(r == c, 1.0, 0.0).astype(jnp.bfloat16)

    def norm_rope(a, head_gain, scale):
        ss = _dot_split(a * a, head_ones)
        y = a * lax.rsqrt(ss * (1.0 / HEAD_DIM) + EPS) * head_gain
        partner = jnp.where(
            first_half, pltpu.roll(y, LANES - HEAD_DIM // 2, 1), pltpu.roll(y, HEAD_DIM // 2, 1)
        )
        y = y * cos + partner * sin_signed
        return y * scale if scale != 1.0 else y

    def project(src_col):
        return jnp.dot(h, w_ref[:, src_col : src_col + CHUNK], preferred_element_type=jnp.float32)

    def store(slot, col, val):
        o_ref[:, slot * SLOT + col : slot * SLOT + col + val.shape[1]] = val.astype(o_ref.dtype)

    q_scale = LOG2E / math.sqrt(HEAD_DIM)
    for j in range(SLOT // CHUNK):
        col = j * CHUNK
        acc = project(SRC_QA + col)
        for half in range(CHUNK // LANES):
            a = acc[:, half * LANES : (half + 1) * LANES]
            store(SLOT_QA, col + half * LANES, norm_rope(a, qgain_ref[...], q_scale))
        store(SLOT_GA, col, _silu(project(SRC_GA + col)))
        store(SLOT_QB, col, project(SRC_QB + col) * q_scale)
        store(SLOT_KB, col, project(SRC_KB + col))
        store(SLOT_VB, col, project(SRC_VB + col))
        store(SLOT_GB, col, _silu(project(SRC_GB + col)))

    acc = project(SRC_KA)
    k = norm_rope(acc[:, :LANES], kgain_ref[...], 1.0)
    v = acc[:, LANES:]
    low = lane < HEAD_DIM
    for base, t in ((0, k), (2 * LANES, v)):
        swapped = pltpu.roll(t, HEAD_DIM, 1)
        store(SLOT_KVA, base, jnp.where(low, t, swapped))
        store(SLOT_KVA, base + LANES, jnp.where(low, swapped, t))


def _inproj(x2, pos2, gain, w_bf16, qgain, kgain, invf, tm):
    n, d = x2.shape
    full = lambda i: (0, 0)
    return pl.pallas_call(
        _inproj_kernel,
        out_shape=jax.ShapeDtypeStruct((n, ACT_WIDTH), jnp.bfloat16),
        grid=(n // tm,),
        in_specs=[
            pl.BlockSpec((tm, d), lambda i: (i, 0)),
            pl.BlockSpec((tm, 1), lambda i: (i, 0)),
            pl.BlockSpec((1, d), full),
            pl.BlockSpec(w_bf16.shape, full),
            pl.BlockSpec((1, LANES), full),
            pl.BlockSpec((1, LANES), full),
            pl.BlockSpec((1, LANES), full),
        ],
        out_specs=pl.BlockSpec((tm, ACT_WIDTH), lambda i: (i, 0)),
        compiler_params=pltpu.CompilerParams(
            dimension_semantics=("arbitrary",), vmem_limit_bytes=VMEM_LIMIT_BYTES
        ),
        name="inproj",
    )(x2, pos2, gain, w_bf16, qgain, kgain, invf)


def _mixer_kernel(
    sinks_ref, x_ref, qa_ref, ga_ref, qb_ref, gb_ref, kb_ref, vb_ref, kva_ref, wout_ref,
    o_ref, y_ref, acc_ref, fail_ref, *, tq,
):
    step = pl.program_id(1)
    row2 = lax.broadcasted_iota(jnp.int32, (2 * BLOCK, LANES), 0)
    lane2 = lax.broadcasted_iota(jnp.int32, (2 * BLOCK, LANES), 1)
    qrow = row2 & (BLOCK - 1)
    own_head = (row2 < BLOCK) == (lane2 < HEAD_DIM)
    in_cur_block = lane2 <= qrow
    low = _lane_iota((1, LANES)) < HEAD_DIM
    qrow_w = lax.broadcasted_iota(jnp.int32, (2 * BLOCK, CHUNK), 0) & (BLOCK - 1)
    key_w = lax.broadcasted_iota(jnp.int32, (2 * BLOCK, CHUNK), 1)
    kr = lax.broadcasted_iota(jnp.int32, (CHUNK, CHUNK), 0)
    kc = lax.broadcasted_iota(jnp.int32, (CHUNK, CHUNK), 1)
    suffix_ones = jnp.where(kr >= kc, 1.0, 0.0).astype(jnp.bfloat16)

    def stack_heads(q_pair):
        q2 = jnp.concatenate([q_pair, q_pair], axis=0)
        return jnp.where(own_head, q2, jnp.zeros_like(q2))

    def unstack_heads(o2):
        return jnp.where(low, o2[:BLOCK], o2[BLOCK:])

    def sb_scores(p, q2, ks, valid):
        k_t = kb_ref[0, pl.ds(ks, CHUNK), p * LANES : (p + 1) * LANES]
        z = lax.dot_general(q2, k_t, _NT, preferred_element_type=jnp.float32)
        return jnp.where(valid, z, MASKED_SCORE)

    def sb_suffix(z):
        return _dot_split(_softplus_log2(z), suffix_ones)

    def sb_accumulate(p, z, upto, ks, first):
        arg = z - upto
        if not first:
            fail = fail_ref[p]
            arg = arg - jnp.concatenate([fail] * (CHUNK // LANES), axis=1)
        w = jnp.exp2(arg).astype(jnp.bfloat16)
        v_t = vb_ref[0, pl.ds(ks, CHUNK), p * LANES : (p + 1) * LANES]
        pv = jnp.dot(w, v_t, preferred_element_type=jnp.float32)
        total = jnp.broadcast_to(upto[:, :1], (2 * BLOCK, LANES))
        if first:
            acc_ref[p] = pv
            fail_ref[p] = total
            return total
        acc_ref[p] += pv
        fail_ref[p] = fail + total
        return fail + total

    def sub_block(sb, carry_unused):
        blk = step * (tq // BLOCK) + sb
        r0 = pl.multiple_of(sb * BLOCK, BLOCK)
        k0 = pl.multiple_of(blk * BLOCK, BLOCK)
        kp = pl.multiple_of(jnp.maximum(blk - 1, 0) * BLOCK, BLOCK)
        has_prev = blk > 0

        swa_pairs = range(SWA_Q_HEADS // 2)
        sb_pairs = range(SB_PAIRS)
        swa_valid = jnp.logical_or(in_cur_block, (jnp.zeros_like(lane2) + blk) > 0)
        window_valid = key_w < (qrow_w + jnp.where(has_prev, BLOCK, 0))
        kv_group = [(2 * p) // (SWA_Q_HEADS // SWA_KV_HEADS) for p in swa_pairs]

        q2s = [stack_heads(qb_ref[0, pl.ds(r0, BLOCK), p * LANES : (p + 1) * LANES]) for p in sb_pairs]
        zs = [sb_scores(p, q2s[p], kp, window_valid) for p in sb_pairs]
        scores = []
        for p in swa_pairs:
            q2 = stack_heads(qa_ref[0, pl.ds(r0, BLOCK), p * LANES : (p + 1) * LANES])
            k_cols = slice(kv_group[p] * LANES, (kv_group[p] + 1) * LANES)
            s_cur = lax.dot_general(
                q2, kva_ref[0, pl.ds(k0, BLOCK), k_cols], _NT, preferred_element_type=jnp.float32
            )
            s_prev = lax.dot_general(
                q2, kva_ref[0, pl.ds(kp, BLOCK), k_cols], _NT, preferred_element_type=jnp.float32
            )
            scores.append((s_cur, s_prev))

        uptos = [sb_suffix(zs[p]) for p in sb_pairs]
        outs = []
        for p in swa_pairs:
            s_cur, s_prev = scores[p]
            s = jnp.where(swa_valid, jnp.where(in_cur_block, s_cur, s_prev), -jnp.inf)
            sink = LOG2E * jnp.where(row2[:, :1] < BLOCK, sinks_ref[0, 2 * p], sinks_ref[0, 2 * p + 1])
            m = jnp.maximum(jnp.max(s, axis=-1, keepdims=True), sink)
            e = jnp.exp2(s - m)
            denom = jnp.sum(e, axis=-1, keepdims=True) + jnp.exp2(sink - m)
            e_cur = jnp.where(in_cur_block, e, 0.0).astype(jnp.bfloat16)
            e_prev = jnp.where(in_cur_block, 0.0, e).astype(jnp.bfloat16)
            v_cols = slice((SWA_KV_HEADS + kv_group[p]) * LANES, (SWA_KV_HEADS + kv_group[p] + 1) * LANES)
            o2 = jnp.dot(
                e_cur, kva_ref[0, pl.ds(k0, BLOCK), v_cols], preferred_element_type=jnp.float32
            ) + jnp.dot(e_prev, kva_ref[0, pl.ds(kp, BLOCK), v_cols], preferred_element_type=jnp.float32)
            outs.append(o2 / denom)

        fail_min = None
        for p in sb_pairs:
            f = sb_accumulate(p, zs[p], uptos[p], kp, first=True)
            fail_min = f if fail_min is None else jnp.minimum(fail_min, f)
        for p in swa_pairs:
            gate = ga_ref[0, pl.ds(r0, BLOCK), p * LANES : (p + 1) * LANES].astype(jnp.float32)
            y_ref[pl.ds(r0, BLOCK), p * LANES : (p + 1) * LANES] = (
                unstack_heads(outs[p]) * gate
            ).astype(y_ref.dtype)

        n_tiles = blk // 2

        def cond(c):
            n, lowest = c
            return jnp.logical_and(n < n_tiles, lowest < SB_DONE_LOG2)

        def body(c):
            n, _ = c
            start = (blk - 1) * BLOCK - CHUNK * (n + 1)
            ks = pl.multiple_of(jnp.maximum(start, 0), BLOCK)
            tile_valid = key_w < (CHUNK + jnp.minimum(start, 0))
            zs = [sb_scores(p, q2s[p], ks, tile_valid) for p in sb_pairs]
            uptos = [sb_suffix(zs[p]) for p in sb_pairs]
            lowest = None
            for p in sb_pairs:
                f = sb_accumulate(p, zs[p], uptos[p], ks, first=False)
                lowest = f if lowest is None else jnp.minimum(lowest, f)
            return n + 1, jnp.min(lowest)

        lax.while_loop(cond, body, (jnp.int32(0), jnp.min(fail_min)))

        for p in range(SB_PAIRS):
            cols = slice(p * LANES, (p + 1) * LANES)
            gate = gb_ref[0, pl.ds(r0, BLOCK), cols].astype(jnp.float32)
            y_ref[pl.ds(r0, BLOCK), SWA_WIDTH + p * LANES : SWA_WIDTH + (p + 1) * LANES] = (
                unstack_heads(acc_ref[p]) * gate
            ).astype(y_ref.dtype)
        return carry_unused

    lax.fori_loop(0, tq // BLOCK, sub_block, 0)
    o_ref[0] = x_ref[0] + jnp.dot(y_ref[...], wout_ref[...], preferred_element_type=jnp.float32)


def _mixer(sinks, x, act, wout_bf16, tq):
    b, s, d = x.shape
    blk = lambda slot: pl.BlockSpec((1, tq, SLOT), lambda bi, i, slot=slot: (bi, i, slot))
    seq = lambda slot: pl.BlockSpec((1, s, SLOT), lambda bi, i, slot=slot: (bi, 0, slot))
    return pl.pallas_call(
        functools.partial(_mixer_kernel, tq=tq),
        out_shape=jax.ShapeDtypeStruct((b, s, d), jnp.float32),
        grid=(b, s // tq),
        in_specs=[
            pl.BlockSpec(memory_space=pltpu.SMEM),
            pl.BlockSpec((1, tq, d), lambda bi, i: (bi, i, 0)),
            blk(SLOT_QA), blk(SLOT_GA), blk(SLOT_QB), blk(SLOT_GB),
            seq(SLOT_KB), seq(SLOT_VB), seq(SLOT_KVA),
            pl.BlockSpec(wout_bf16.shape, lambda bi, i: (0, 0)),
        ],
        out_specs=pl.BlockSpec((1, tq, d), lambda bi, i: (bi, i, 0)),
        scratch_shapes=[
            pltpu.VMEM((tq, MIX_WIDTH), jnp.bfloat16),
            pltpu.VMEM((SB_PAIRS, 2 * BLOCK, LANES), jnp.float32),
            pltpu.VMEM((SB_PAIRS, 2 * BLOCK, LANES), jnp.float32),
        ],
        compiler_params=pltpu.CompilerParams(
            dimension_semantics=("arbitrary", "arbitrary"), vmem_limit_bytes=VMEM_LIMIT_BYTES
        ),
        name="mixer",
    )(sinks, x, act, act, act, act, act, act, act, wout_bf16)


def kernel(x, positions, norm_gain, w_in, q_norm_gain, k_norm_gain, sinks, w_out):
    b, s, d = x.shape
    assert s % CHUNK == 0 and s >= CHUNK
    depth = w_in.shape[0]
    half = HEAD_DIM // 2
    inv_freq = ROPE_THETA ** (-jnp.arange(half, dtype=jnp.float32) * 2.0 / HEAD_DIM)
    invf = jnp.tile(inv_freq, LANES // half).reshape(1, LANES)
    pos2 = positions.reshape(b * s, 1)
    for l in range(depth):
        act = _inproj(
            x.reshape(b * s, d),
            pos2,
            norm_gain[l].reshape(1, d),
            w_in[l].astype(jnp.bfloat16),
            jnp.tile(q_norm_gain[l], LANES // HEAD_DIM).reshape(1, LANES),
            jnp.tile(k_norm_gain[l], LANES // HEAD_DIM).reshape(1, LANES),
            invf,
            tm=512,
        )
        x = _mixer(
            sinks[l].reshape(1, SWA_Q_HEADS),
            x,
            act.reshape(b, s, ACT_WIDTH),
            w_out[l].astype(jnp.bfloat16),
            tq=512,
        )
    return x
```

```python
import functools
import math

import jax
import jax.numpy as jnp
from jax import lax
from jax.experimental import pallas as pl
from jax.experimental.pallas import tpu as pltpu

HEAD_DIM = 64
SWA_Q_HEADS = 8
SWA_KV_HEADS = 2
SB_HEADS = 8
BLOCK = 128
ROPE_THETA = 10000.0
EPS = 1e-6
LANES = 128
SUBLANES = 8
CHUNK = 256
LOG2E = math.log2(math.e)

SWA_WIDTH = SWA_Q_HEADS * HEAD_DIM
SWA_KV_WIDTH = SWA_KV_HEADS * HEAD_DIM
SB_WIDTH = SB_HEADS * HEAD_DIM
MIX_WIDTH = SWA_WIDTH + SB_WIDTH
SB_PAIRS = SB_HEADS // 2

SRC_QA = 0
SRC_KA = SRC_QA + SWA_WIDTH
SRC_VA = SRC_KA + SWA_KV_WIDTH
SRC_GA = SRC_VA + SWA_KV_WIDTH
SRC_QB = SRC_GA + SWA_WIDTH
SRC_KB = SRC_QB + SB_WIDTH
SRC_VB = SRC_KB + SB_WIDTH
SRC_GB = SRC_VB + SB_WIDTH

SLOT = 512
SLOT_QA, SLOT_GA, SLOT_QB, SLOT_KB, SLOT_VB, SLOT_GB, SLOT_KVA = range(7)
ACT_WIDTH = 7 * SLOT

VMEM_LIMIT_BYTES = 56 * 1024 * 1024

SB_DONE_LOG2 = 128.0
MASKED_SCORE = -1e30

_NT = (((1,), (1,)), ((), ()))


def _lane_iota(shape):
    return lax.broadcasted_iota(jnp.int32, shape, len(shape) - 1)


def _split_bf16(v):
    hi = v.astype(jnp.bfloat16)
    lo = (v - hi.astype(jnp.float32)).astype(jnp.bfloat16)
    return hi, lo


def _dot_split(v, rhs):
    hi, lo = _split_bf16(v)
    return jnp.dot(hi, rhs, preferred_element_type=jnp.float32) + jnp.dot(
        lo, rhs, preferred_element_type=jnp.float32
    )


def _silu(g):
    return g * (1.0 / (1.0 + jnp.exp(-g)))


def _softplus_log2(z2):
    return jnp.maximum(z2, 0.0) + jnp.log2(1.0 + jnp.exp2(-jnp.abs(z2)))


def _inproj_kernel(x_ref, pos_ref, gain_ref, w_ref, qgain_ref, kgain_ref, invf_ref, o_ref):
    x = x_ref[...]
    ms = jnp.mean(x * x, axis=-1, keepdims=True)
    h = (x * lax.rsqrt(ms + EPS) * gain_ref[...]).astype(jnp.bfloat16)

    lane = _lane_iota((1, LANES))
    first_half = (lane % HEAD_DIM) < (HEAD_DIM // 2)
    n_groups = LANES // (HEAD_DIM // 2)
    quarter = x.shape[0] // n_groups
    group = lane // (HEAD_DIM // 2)
    pos = pos_ref[...].astype(jnp.float32)
    packed_pos = pos[(n_groups - 1) * quarter :]
    for g in reversed(range(n_groups - 1)):
        packed_pos = jnp.where(group == g, pos[g * quarter : (g + 1) * quarter], packed_pos)
    packed_ang = packed_pos * invf_ref[...]

    def spread(packed):
        rolled = [packed] + [pltpu.roll(packed, s * (HEAD_DIM // 2), 1) for s in range(1, n_groups)]
        quarters = []
        for g in range(n_groups):
            t = rolled[(n_groups - 1 - g) % n_groups]
            for j in reversed(range(n_groups - 1)):
                t = jnp.where(group == j, rolled[(j - g) % n_groups], t)
            quarters.append(t)
        return jnp.concatenate(quarters, axis=0)

    cos = spread(jnp.cos(packed_ang))
    sin_signed = spread(jnp.sin(packed_ang)) * jnp.where(first_half, -1.0, 1.0)

    r = lax.broadcasted_iota(jnp.int32, (LANES, LANES), 0) // HEAD_DIM
    c = lax.broadcasted_iota(jnp.int32, (LANES, LANES), 1) // HEAD_DIM
    head_ones = jnp.where(r == c, 1.0, 0.0).astype(jnp.bfloat16)

    def norm_rope(a, head_gain, scale):
        ss = _dot_split(a * a, head_ones)
        y = a * lax.rsqrt(ss * (1.0 / HEAD_DIM) + EPS) * head_gain
        partner = jnp.where(
            first_half, pltpu.roll(y, LANES - HEAD_DIM // 2, 1), pltpu.roll(y, HEAD_DIM // 2, 1)
        )
        y = y * cos + partner * sin_signed
        return y * scale if scale != 1.0 else y

    def project(src_col):
        return jnp.dot(h, w_ref[:, src_col : src_col + CHUNK], preferred_element_type=jnp.float32)

    def store(slot, col, val):
        o_ref[:, slot * SLOT + col : slot * SLOT + col + val.shape[1]] = val.astype(o_ref.dtype)

    q_scale = LOG2E / math.sqrt(HEAD_DIM)
    def swa_queries(acc, col):
        for half in range(CHUNK // LANES):
            a = acc[:, half * LANES : (half + 1) * LANES]
            store(SLOT_QA, col + half * LANES, norm_rope(a, qgain_ref[...], q_scale))

    def swa_keys_values(acc):
        k = norm_rope(acc[:, :LANES], kgain_ref[...], 1.0)
        v = acc[:, LANES:]
        low = lane < HEAD_DIM
        for base, t in ((0, k), (2 * LANES, v)):
            swapped = pltpu.roll(t, HEAD_DIM, 1)
            store(SLOT_KVA, base, jnp.where(low, t, swapped))
            store(SLOT_KVA, base + LANES, jnp.where(low, swapped, t))

    first, second = 0, CHUNK
    store(SLOT_GA, first, _silu(project(SRC_GA + first)))
    store(SLOT_QB, first, project(SRC_QB + first) * q_scale)
    qa_first = project(SRC_QA + first)
    store(SLOT_KB, first, project(SRC_KB + first))
    swa_queries(qa_first, first)
    store(SLOT_VB, first, project(SRC_VB + first))
    qa_second = project(SRC_QA + second)
    store(SLOT_GB, first, _silu(project(SRC_GB + first)))
    swa_queries(qa_second, second)
    store(SLOT_GA, second, _silu(project(SRC_GA + second)))
    kva = project(SRC_KA)
    store(SLOT_QB, second, project(SRC_QB + second) * q_scale)
    swa_keys_values(kva)
    store(SLOT_GB, second, _silu(project(SRC_GB + second)))
    store(SLOT_KB, second, project(SRC_KB + second))
    store(SLOT_VB, second, project(SRC_VB + second))


def _inproj(x2, pos2, gain, w_bf16, qgain, kgain, invf, tm):
    n, d = x2.shape
    full = lambda i: (0, 0)
    return pl.pallas_call(
        _inproj_kernel,
        out_shape=jax.ShapeDtypeStruct((n, ACT_WIDTH), jnp.bfloat16),
        grid=(n // tm,),
        in_specs=[
            pl.BlockSpec((tm, d), lambda i: (i, 0)),
            pl.BlockSpec((tm, 1), lambda i: (i, 0)),
            pl.BlockSpec((1, d), full),
            pl.BlockSpec(w_bf16.shape, full),
            pl.BlockSpec((1, LANES), full),
            pl.BlockSpec((1, LANES), full),
            pl.BlockSpec((1, LANES), full),
        ],
        out_specs=pl.BlockSpec((tm, ACT_WIDTH), lambda i: (i, 0)),
        compiler_params=pltpu.CompilerParams(
            dimension_semantics=("arbitrary",), vmem_limit_bytes=VMEM_LIMIT_BYTES
        ),
        name="inproj",
    )(x2, pos2, gain, w_bf16, qgain, kgain, invf)


def _mixer_kernel(
    sinks_ref, x_ref, qa_ref, ga_ref, qb_ref, gb_ref, kb_ref, vb_ref, kva_ref, wout_ref,
    o_ref, y_ref, acc_ref, fail_ref, lowest_ref, *, tq,
):
    step = pl.program_id(1)
    n_sub = tq // BLOCK
    row2 = lax.broadcasted_iota(jnp.int32, (2 * BLOCK, LANES), 0)
    lane2 = lax.broadcasted_iota(jnp.int32, (2 * BLOCK, LANES), 1)
    qrow = row2 & (BLOCK - 1)
    own_head = (row2 < BLOCK) == (lane2 < HEAD_DIM)
    in_cur_block = lane2 <= qrow
    low = _lane_iota((1, LANES)) < HEAD_DIM
    qrow_w = lax.broadcasted_iota(jnp.int32, (2 * BLOCK, CHUNK), 0) & (BLOCK - 1)
    key_w = lax.broadcasted_iota(jnp.int32, (2 * BLOCK, CHUNK), 1)
    kr = lax.broadcasted_iota(jnp.int32, (CHUNK, CHUNK), 0)
    kc = lax.broadcasted_iota(jnp.int32, (CHUNK, CHUNK), 1)
    suffix_ones = jnp.where(kr >= kc, 1.0, 0.0).astype(jnp.bfloat16)

    def stack_heads(q_pair):
        q2 = jnp.concatenate([q_pair, q_pair], axis=0)
        return jnp.where(own_head, q2, jnp.zeros_like(q2))

    def unstack_heads(o2):
        return jnp.where(low, o2[:BLOCK], o2[BLOCK:])

    def sb_scores(p, q2, ks, valid):
        k_t = kb_ref[0, pl.ds(ks, CHUNK), p * LANES : (p + 1) * LANES]
        z = lax.dot_general(q2, k_t, _NT, preferred_element_type=jnp.float32)
        return jnp.where(valid, z, MASKED_SCORE)

    def sb_suffix(z):
        return jnp.dot(
            _softplus_log2(z).astype(jnp.bfloat16), suffix_ones, preferred_element_type=jnp.float32
        )

    def sb_accumulate(sb, p, z, upto, ks, first):
        arg = z - upto
        if not first:
            fail = fail_ref[sb, p]
            arg = arg - jnp.concatenate([fail] * (CHUNK // LANES), axis=1)
        w = jnp.exp2(arg).astype(jnp.bfloat16)
        v_t = vb_ref[0, pl.ds(ks, CHUNK), p * LANES : (p + 1) * LANES]
        pv = jnp.dot(w, v_t, preferred_element_type=jnp.float32)
        total = jnp.broadcast_to(upto[:, :1], (2 * BLOCK, LANES))
        if first:
            acc_ref[sb, p] = pv
            fail_ref[sb, p] = total
            return total
        acc_ref[sb, p] += pv
        fail_ref[sb, p] = fail + total
        return fail + total

    swa_pairs = range(SWA_Q_HEADS // 2)
    sb_pairs = range(SB_PAIRS)
    kv_group = [(2 * p) // (SWA_Q_HEADS // SWA_KV_HEADS) for p in swa_pairs]

    def block_index(sb):
        return step * n_sub + sb

    def window_start(blk):
        return pl.multiple_of(jnp.maximum(blk - 1, 0) * BLOCK, BLOCK)

    def sb_queries(r0):
        return [stack_heads(qb_ref[0, pl.ds(r0, BLOCK), p * LANES : (p + 1) * LANES]) for p in sb_pairs]

    def stage_scores(sb):
        blk = block_index(sb)
        kp = window_start(blk)
        r0 = sb * BLOCK
        window_valid = key_w < (qrow_w + jnp.where(blk > 0, BLOCK, 0))
        zs = [sb_scores(p, q2, kp, window_valid) for p, q2 in enumerate(sb_queries(r0))]
        scores = []
        for p in swa_pairs:
            q2 = stack_heads(qa_ref[0, pl.ds(r0, BLOCK), p * LANES : (p + 1) * LANES])
            k_cols = slice(kv_group[p] * LANES, (kv_group[p] + 1) * LANES)
            scores.append(
                lax.dot_general(
                    q2, kva_ref[0, pl.ds(kp, 2 * BLOCK), k_cols], _NT, preferred_element_type=jnp.float32
                )
            )
        return dict(blk=blk, kp=kp, r0=r0, zs=zs, scores=scores)

    def stage_reduce(st):
        st["uptos"] = [sb_suffix(z) for z in st["zs"]]
        has_prev_cells = (jnp.zeros_like(lane2) + st["blk"]) > 0
        swa_valid = jnp.logical_or(in_cur_block, has_prev_cells)
        take_first = in_cur_block != has_prev_cells
        st["probs"] = []
        for p in swa_pairs:
            s_first, s_second = st["scores"][p][:, :BLOCK], st["scores"][p][:, BLOCK:]
            s = jnp.where(swa_valid, jnp.where(take_first, s_first, s_second), -jnp.inf)
            sink = jnp.where(
                row2[:, :1] < BLOCK, LOG2E * sinks_ref[0, 2 * p], LOG2E * sinks_ref[0, 2 * p + 1]
            )
            m = jnp.max(s, axis=-1, keepdims=True)
            e = jnp.exp2(s - m)
            denom = jnp.sum(e, axis=-1, keepdims=True) + jnp.exp2(sink - m)
            e_both = jnp.concatenate(
                [jnp.where(take_first, e, 0.0), jnp.where(take_first, 0.0, e)], axis=1
            ).astype(jnp.bfloat16)
            st["probs"].append((e_both, denom))

    def stage_values(sb, st):
        kp, r0 = st["kp"], st["r0"]
        for p in swa_pairs:
            e_both, denom = st["probs"][p]
            v_cols = slice((SWA_KV_HEADS + kv_group[p]) * LANES, (SWA_KV_HEADS + kv_group[p] + 1) * LANES)
            o2 = jnp.dot(e_both, kva_ref[0, pl.ds(kp, 2 * BLOCK), v_cols], preferred_element_type=jnp.float32)
            gate = ga_ref[0, pl.ds(r0, BLOCK), p * LANES : (p + 1) * LANES].astype(jnp.float32)
            y_ref[pl.ds(r0, BLOCK), p * LANES : (p + 1) * LANES] = (
                unstack_heads(o2 / denom) * gate
            ).astype(y_ref.dtype)
        lowest = None
        for p in sb_pairs:
            f = sb_accumulate(sb, p, st["zs"][p], st["uptos"][p], kp, first=True)
            lowest = f if lowest is None else jnp.minimum(lowest, f)
        lowest = jnp.where(st["blk"] >= 2, lowest, jnp.inf)
        lowest_ref[sb] = jnp.min(lowest.reshape(-1, SUBLANES, LANES), axis=0)

    stages = [None] * n_sub
    stages[0] = stage_scores(0)
    for sb in range(n_sub):
        stage_reduce(stages[sb])
        if sb + 1 < n_sub:
            stages[sb + 1] = stage_scores(sb + 1)
        stage_values(sb, stages[sb])
        stages[sb] = None

    @pl.when(jnp.min(lowest_ref[...]) < SB_DONE_LOG2)
    def _():
        def earlier_tiles(sb, carry):
            blk = block_index(sb)
            q2s = sb_queries(pl.multiple_of(sb * BLOCK, BLOCK))
            n_tiles = blk // 2

            def cond(c):
                n, lowest = c
                return jnp.logical_and(n < n_tiles, lowest < SB_DONE_LOG2)

            def body(c):
                n, _ = c
                start = (blk - 1) * BLOCK - CHUNK * (n + 1)
                ks = pl.multiple_of(jnp.maximum(start, 0), BLOCK)
                tile_valid = key_w < (CHUNK + jnp.minimum(start, 0))
                zs = [sb_scores(p, q2s[p], ks, tile_valid) for p in sb_pairs]
                uptos = [sb_suffix(z) for z in zs]
                lowest = None
                for p in sb_pairs:
                    f = sb_accumulate(sb, p, zs[p], uptos[p], ks, first=False)
                    lowest = f if lowest is None else jnp.minimum(lowest, f)
                return n + 1, jnp.min(lowest)

            lax.while_loop(cond, body, (jnp.int32(0), jnp.min(lowest_ref[sb])))
            return carry

        lax.fori_loop(0, n_sub, earlier_tiles, 0)

    for sb in range(n_sub):
        rows = slice(sb * BLOCK, (sb + 1) * BLOCK)
        for p in sb_pairs:
            gate = gb_ref[0, rows, p * LANES : (p + 1) * LANES].astype(jnp.float32)
            y_ref[rows, SWA_WIDTH + p * LANES : SWA_WIDTH + (p + 1) * LANES] = (
                unstack_heads(acc_ref[sb, p]) * gate
            ).astype(y_ref.dtype)
    o_ref[0] = x_ref[0] + jnp.dot(y_ref[...], wout_ref[...], preferred_element_type=jnp.float32)


def _mixer(sinks, x, act, wout_bf16, tq):
    b, s, d = x.shape
    blk = lambda slot: pl.BlockSpec((1, tq, SLOT), lambda bi, i, slot=slot: (bi, i, slot))
    seq = lambda slot: pl.BlockSpec((1, s, SLOT), lambda bi, i, slot=slot: (bi, 0, slot))
    return pl.pallas_call(
        functools.partial(_mixer_kernel, tq=tq),
        out_shape=jax.ShapeDtypeStruct((b, s, d), jnp.float32),
        grid=(b, s // tq),
        in_specs=[
            pl.BlockSpec(memory_space=pltpu.SMEM),
            pl.BlockSpec((1, tq, d), lambda bi, i: (bi, i, 0)),
            blk(SLOT_QA), blk(SLOT_GA), blk(SLOT_QB), blk(SLOT_GB),
            seq(SLOT_KB), seq(SLOT_VB), seq(SLOT_KVA),
            pl.BlockSpec(wout_bf16.shape, lambda bi, i: (0, 0)),
        ],
        out_specs=pl.BlockSpec((1, tq, d), lambda bi, i: (bi, i, 0)),
        scratch_shapes=[
            pltpu.VMEM((tq, MIX_WIDTH), jnp.bfloat16),
            pltpu.VMEM((tq // BLOCK, SB_PAIRS, 2 * BLOCK, LANES), jnp.float32),
            pltpu.VMEM((tq // BLOCK, SB_PAIRS, 2 * BLOCK, LANES), jnp.float32),
            pltpu.VMEM((tq // BLOCK, SUBLANES, LANES), jnp.float32),
        ],
        compiler_params=pltpu.CompilerParams(
            dimension_semantics=("arbitrary", "arbitrary"), vmem_limit_bytes=VMEM_LIMIT_BYTES
        ),
        name="mixer",
    )(sinks, x, act, act, act, act, act, act, act, wout_bf16)


def kernel(x, positions, norm_gain, w_in, q_norm_gain, k_norm_gain, sinks, w_out):
    b, s, d = x.shape
    assert s % CHUNK == 0 and s >= CHUNK
    depth = w_in.shape[0]
    half = HEAD_DIM // 2
    inv_freq = ROPE_THETA ** (-jnp.arange(half, dtype=jnp.float32) * 2.0 / HEAD_DIM)
    invf = jnp.tile(inv_freq, LANES // half).reshape(1, LANES)
    pos2 = positions.reshape(b * s, 1)
    for l in range(depth):
        act = _inproj(
            x.reshape(b * s, d),
            pos2,
            norm_gain[l].reshape(1, d),
            w_in[l].astype(jnp.bfloat16),
            jnp.tile(q_norm_gain[l], LANES // HEAD_DIM).reshape(1, LANES),
            jnp.tile(k_norm_gain[l], LANES // HEAD_DIM).reshape(1, LANES),
            invf,
            tm=512,
        )
        x = _mixer(
            sinks[l].reshape(1, SWA_Q_HEADS),
            x,
            act.reshape(b, s, ACT_WIDTH),
            w_out[l].astype(jnp.bfloat16),
            tq=512,
        )
    return x
```

```python
import functools
import math

import jax
import jax.numpy as jnp
from jax import lax
from jax.experimental import pallas as pl
from jax.experimental.pallas import tpu as pltpu

HEAD_DIM = 64
SWA_Q_HEADS = 8
SWA_KV_HEADS = 2
SB_HEADS = 8
BLOCK = 128
ROPE_THETA = 10000.0
EPS = 1e-6
LANES = 128
SUBLANES = 8
CHUNK = 256
LOG2E = math.log2(math.e)

SWA_WIDTH = SWA_Q_HEADS * HEAD_DIM
SWA_KV_WIDTH = SWA_KV_HEADS * HEAD_DIM
SB_WIDTH = SB_HEADS * HEAD_DIM
MIX_WIDTH = SWA_WIDTH + SB_WIDTH
SWA_PAIRS = SWA_Q_HEADS // 2
SB_PAIRS = SB_HEADS // 2

SRC_QA = 0
SRC_KA = SRC_QA + SWA_WIDTH
SRC_VA = SRC_KA + SWA_KV_WIDTH
SRC_GA = SRC_VA + SWA_KV_WIDTH
SRC_QB = SRC_GA + SWA_WIDTH
SRC_KB = SRC_QB + SB_WIDTH
SRC_VB = SRC_KB + SB_WIDTH
SRC_GB = SRC_VB + SB_WIDTH

SLOT = 512
SLOT_YA, SLOT_QB, SLOT_KB, SLOT_VB, SLOT_GB = range(5)
ACT_WIDTH = 5 * SLOT

VMEM_LIMIT_BYTES = 56 * 1024 * 1024

SB_DONE_LOG2 = 128.0
MASKED_SCORE = -1e30

_NT = (((1,), (1,)), ((), ()))


def _lane_iota(shape):
    return lax.broadcasted_iota(jnp.int32, shape, len(shape) - 1)


def _split_bf16(v):
    hi = v.astype(jnp.bfloat16)
    lo = (v - hi.astype(jnp.float32)).astype(jnp.bfloat16)
    return hi, lo


def _dot_split(v, rhs):
    hi, lo = _split_bf16(v)
    return jnp.dot(hi, rhs, preferred_element_type=jnp.float32) + jnp.dot(
        lo, rhs, preferred_element_type=jnp.float32
    )


def _silu(g):
    return g * (1.0 / (1.0 + jnp.exp(-g)))


def _softplus_log2(z2):
    return jnp.maximum(z2, 0.0) + jnp.log2(1.0 + jnp.exp2(-jnp.abs(z2)))


def _pair_masks():
    row2 = lax.broadcasted_iota(jnp.int32, (2 * BLOCK, LANES), 0)
    lane2 = lax.broadcasted_iota(jnp.int32, (2 * BLOCK, LANES), 1)
    own_head = (row2 < BLOCK) == (lane2 < HEAD_DIM)
    low = _lane_iota((1, LANES)) < HEAD_DIM

    def stack_heads(q_pair):
        q2 = jnp.concatenate([q_pair, q_pair], axis=0)
        return jnp.where(own_head, q2, jnp.zeros_like(q2))

    def unstack_heads(o2):
        return jnp.where(low, o2[:BLOCK], o2[BLOCK:])

    return row2, lane2, stack_heads, unstack_heads


def _inproj_kernel(
    sinks_ref, x_ref, pos_ref, gain_ref, w_ref, qgain_ref, kgain_ref, invf_ref,
    o_ref, qa_ref, ga_ref, kv_ref, *, tiles_per_seq,
):
    tm = x_ref.shape[0]
    first_tile = (pl.program_id(0) % tiles_per_seq) == 0

    @pl.when(first_tile)
    def _():
        kv_ref[0:BLOCK, :] = jnp.zeros((BLOCK, kv_ref.shape[1]), kv_ref.dtype)

    x = x_ref[...]
    ms = jnp.mean(x * x, axis=-1, keepdims=True)
    h = (x * lax.rsqrt(ms + EPS) * gain_ref[...]).astype(jnp.bfloat16)

    lane = _lane_iota((1, LANES))
    first_half = (lane % HEAD_DIM) < (HEAD_DIM // 2)
    n_groups = LANES // (HEAD_DIM // 2)
    quarter = tm // n_groups
    group = lane // (HEAD_DIM // 2)
    pos = pos_ref[...].astype(jnp.float32)
    packed_pos = pos[(n_groups - 1) * quarter :]
    for g in reversed(range(n_groups - 1)):
        packed_pos = jnp.where(group == g, pos[g * quarter : (g + 1) * quarter], packed_pos)
    packed_ang = packed_pos * invf_ref[...]

    def spread(packed):
        rolled = [packed] + [pltpu.roll(packed, s * (HEAD_DIM // 2), 1) for s in range(1, n_groups)]
        quarters = []
        for g in range(n_groups):
            t = rolled[(n_groups - 1 - g) % n_groups]
            for j in reversed(range(n_groups - 1)):
                t = jnp.where(group == j, rolled[(j - g) % n_groups], t)
            quarters.append(t)
        return jnp.concatenate(quarters, axis=0)

    cos = spread(jnp.cos(packed_ang))
    sin_signed = spread(jnp.sin(packed_ang)) * jnp.where(first_half, -1.0, 1.0)

    r = lax.broadcasted_iota(jnp.int32, (LANES, LANES), 0) // HEAD_DIM
    c = lax.broadcasted_iota(jnp.int32, (LANES, LANES), 1) // HEAD_DIM
    head_ones = jnp.where(r == c, 1.0, 0.0).astype(jnp.bfloat16)

    def norm_rope(a, head_gain, scale):
        ss = _dot_split(a * a, head_ones)
        y = a * lax.rsqrt(ss * (1.0 / HEAD_DIM) + EPS) * head_gain
        partner = jnp.where(
            first_half, pltpu.roll(y, LANES - HEAD_DIM // 2, 1), pltpu.roll(y, HEAD_DIM // 2, 1)
        )
        y = y * cos + partner * sin_signed
        return y * scale if scale != 1.0 else y

    def project(src_col):
        return jnp.dot(h, w_ref[:, src_col : src_col + CHUNK], preferred_element_type=jnp.float32)

    def store(slot, col, val):
        o_ref[:, slot * SLOT + col : slot * SLOT + col + val.shape[1]] = val.astype(o_ref.dtype)

    q_scale = LOG2E / math.sqrt(HEAD_DIM)

    def swa_queries(acc, col):
        for half in range(CHUNK // LANES):
            a = acc[:, half * LANES : (half + 1) * LANES]
            lanes = slice(col + half * LANES, col + (half + 1) * LANES)
            qa_ref[:, lanes] = norm_rope(a, qgain_ref[...], q_scale).astype(qa_ref.dtype)

    def swa_keys_values(acc):
        k = norm_rope(acc[:, :LANES], kgain_ref[...], 1.0)
        v = acc[:, LANES:]
        low = lane < HEAD_DIM
        for base, t in ((0, k), (2 * LANES, v)):
            swapped = pltpu.roll(t, HEAD_DIM, 1)
            kv_ref[BLOCK:, base : base + LANES] = jnp.where(low, t, swapped).astype(kv_ref.dtype)
            kv_ref[BLOCK:, base + LANES : base + 2 * LANES] = jnp.where(low, swapped, t).astype(kv_ref.dtype)

    row2, lane2, stack_heads, unstack_heads = _pair_masks()
    in_cur_block = lane2 <= (row2 & (BLOCK - 1))
    kv_group = [(2 * p) // (SWA_Q_HEADS // SWA_KV_HEADS) for p in range(SWA_PAIRS)]

    def swa_scores(j):
        scores = []
        for p in range(SWA_PAIRS):
            q2 = stack_heads(qa_ref[j * BLOCK : (j + 1) * BLOCK, p * LANES : (p + 1) * LANES])
            k_win = kv_ref[j * BLOCK : (j + 2) * BLOCK, kv_group[p] * LANES : (kv_group[p] + 1) * LANES]
            scores.append(lax.dot_general(q2, k_win, _NT, preferred_element_type=jnp.float32))
        return scores

    def swa_softmax(j, scores):
        probs = []
        for p in range(SWA_PAIRS):
            s = jnp.where(in_cur_block, scores[p][:, BLOCK:], scores[p][:, :BLOCK])
            if j == 0:
                has_prev_cells = jnp.logical_not((jnp.zeros_like(lane2) + first_tile.astype(jnp.int32)) > 0)
                s = jnp.where(jnp.logical_or(in_cur_block, has_prev_cells), s, -jnp.inf)
            sink = jnp.where(
                row2[:, :1] < BLOCK, LOG2E * sinks_ref[0, 2 * p], LOG2E * sinks_ref[0, 2 * p + 1]
            )
            m = jnp.max(s, axis=-1, keepdims=True)
            e = jnp.exp2(s - m)
            denom = jnp.sum(e, axis=-1, keepdims=True) + jnp.exp2(sink - m)
            e_both = jnp.concatenate(
                [jnp.where(in_cur_block, 0.0, e), jnp.where(in_cur_block, e, 0.0)], axis=1
            ).astype(jnp.bfloat16)
            probs.append((e_both, denom))
        return probs

    def swa_values(j, probs):
        rows = slice(j * BLOCK, (j + 1) * BLOCK)
        for p in range(SWA_PAIRS):
            e_both, denom = probs[p]
            v_cols = slice((SWA_KV_HEADS + kv_group[p]) * LANES, (SWA_KV_HEADS + kv_group[p] + 1) * LANES)
            o2 = jnp.dot(e_both, kv_ref[j * BLOCK : (j + 2) * BLOCK, v_cols], preferred_element_type=jnp.float32)
            gate = ga_ref[rows, p * LANES : (p + 1) * LANES].astype(jnp.float32)
            o_ref[rows, SLOT_YA * SLOT + p * LANES : SLOT_YA * SLOT + (p + 1) * LANES] = (
                unstack_heads(o2 / denom) * gate
            ).astype(o_ref.dtype)

    first, second = 0, CHUNK
    qa_first = project(SRC_QA + first)
    kva = project(SRC_KA)
    qa_second = project(SRC_QA + second)
    ga_ref[:, first : first + CHUNK] = _silu(project(SRC_GA + first)).astype(ga_ref.dtype)
    swa_queries(qa_first, first)
    ga_ref[:, second : second + CHUNK] = _silu(project(SRC_GA + second)).astype(ga_ref.dtype)
    swa_keys_values(kva)
    store(SLOT_QB, first, project(SRC_QB + first) * q_scale)
    swa_queries(qa_second, second)
    store(SLOT_QB, second, project(SRC_QB + second) * q_scale)

    plain_chunks = [
        (SLOT_KB, SRC_KB, first, None), (SLOT_KB, SRC_KB, second, None),
        (SLOT_VB, SRC_VB, first, None), (SLOT_VB, SRC_VB, second, None),
        (SLOT_GB, SRC_GB, first, _silu), (SLOT_GB, SRC_GB, second, _silu),
    ]

    def next_plain_chunk():
        if plain_chunks:
            slot, src, col, post = plain_chunks.pop(0)
            val = project(src + col)
            store(slot, col, post(val) if post else val)

    n_blk = tm // BLOCK
    scores = swa_scores(0)
    for j in range(n_blk):
        next_plain_chunk()
        probs = swa_softmax(j, scores)
        if j + 1 < n_blk:
            scores = swa_scores(j + 1)
        next_plain_chunk()
        swa_values(j, probs)
    while plain_chunks:
        next_plain_chunk()

    kv_ref[0:BLOCK, :] = kv_ref[tm : tm + BLOCK, :]


def _inproj(sinks, x2, pos2, gain, w_bf16, qgain, kgain, invf, tm, seq_len):
    n, d = x2.shape
    full = lambda i: (0, 0)
    return pl.pallas_call(
        functools.partial(_inproj_kernel, tiles_per_seq=seq_len // tm),
        out_shape=jax.ShapeDtypeStruct((n, ACT_WIDTH), jnp.bfloat16),
        grid=(n // tm,),
        in_specs=[
            pl.BlockSpec(memory_space=pltpu.SMEM),
            pl.BlockSpec((tm, d), lambda i: (i, 0)),
            pl.BlockSpec((tm, 1), lambda i: (i, 0)),
            pl.BlockSpec((1, d), full),
            pl.BlockSpec(w_bf16.shape, full),
            pl.BlockSpec((1, LANES), full),
            pl.BlockSpec((1, LANES), full),
            pl.BlockSpec((1, LANES), full),
        ],
        out_specs=pl.BlockSpec((tm, ACT_WIDTH), lambda i: (i, 0)),
        scratch_shapes=[
            pltpu.VMEM((tm, SWA_WIDTH), jnp.bfloat16),
            pltpu.VMEM((tm, SWA_WIDTH), jnp.bfloat16),
            pltpu.VMEM((BLOCK + tm, 2 * SWA_KV_HEADS * LANES), jnp.bfloat16),
        ],
        compiler_params=pltpu.CompilerParams(
            dimension_semantics=("arbitrary",), vmem_limit_bytes=VMEM_LIMIT_BYTES
        ),
        name="inproj",
    )(sinks, x2, pos2, gain, w_bf16, qgain, kgain, invf)


def _mixer_kernel(
    x_ref, ya_ref, qb_ref, gb_ref, kb_ref, vb_ref, wout_ref,
    o_ref, yb_ref, acc_ref, fail_ref, lowest_ref, *, tq,
):
    step = pl.program_id(1)
    n_sub = tq // BLOCK
    _, _, stack_heads, unstack_heads = _pair_masks()
    qrow_w = lax.broadcasted_iota(jnp.int32, (2 * BLOCK, CHUNK), 0) & (BLOCK - 1)
    key_w = lax.broadcasted_iota(jnp.int32, (2 * BLOCK, CHUNK), 1)
    kr = lax.broadcasted_iota(jnp.int32, (CHUNK, CHUNK), 0)
    kc = lax.broadcasted_iota(jnp.int32, (CHUNK, CHUNK), 1)
    suffix_ones = jnp.where(kr >= kc, 1.0, 0.0).astype(jnp.bfloat16)
    sb_pairs = range(SB_PAIRS)

    def sb_scores(p, q2, ks, valid):
        k_t = kb_ref[0, pl.ds(ks, CHUNK), p * LANES : (p + 1) * LANES]
        z = lax.dot_general(q2, k_t, _NT, preferred_element_type=jnp.float32)
        return jnp.where(valid, z, MASKED_SCORE)

    def sb_suffix(z):
        return jnp.dot(
            _softplus_log2(z).astype(jnp.bfloat16), suffix_ones, preferred_element_type=jnp.float32
        )

    def sb_accumulate(sb, p, z, upto, ks, first):
        arg = z - upto
        if not first:
            fail = fail_ref[sb, p]
            arg = arg - jnp.concatenate([fail] * (CHUNK // LANES), axis=1)
        w = jnp.exp2(arg).astype(jnp.bfloat16)
        v_t = vb_ref[0, pl.ds(ks, CHUNK), p * LANES : (p + 1) * LANES]
        pv = jnp.dot(w, v_t, preferred_element_type=jnp.float32)
        total = jnp.broadcast_to(upto[:, :1], (2 * BLOCK, LANES))
        if first:
            acc_ref[sb, p] = pv
            fail_ref[sb, p] = total
            return total
        acc_ref[sb, p] += pv
        fail_ref[sb, p] = fail + total
        return fail + total

    def block_index(sb):
        return step * n_sub + sb

    def sb_queries(r0):
        return [stack_heads(qb_ref[0, pl.ds(r0, BLOCK), p * LANES : (p + 1) * LANES]) for p in sb_pairs]

    def stage_scores(sb):
        blk = block_index(sb)
        kp = pl.multiple_of(jnp.maximum(blk - 1, 0) * BLOCK, BLOCK)
        window_valid = key_w < (qrow_w + jnp.where(blk > 0, BLOCK, 0))
        zs = [sb_scores(p, q2, kp, window_valid) for p, q2 in enumerate(sb_queries(sb * BLOCK))]
        return dict(blk=blk, kp=kp, zs=zs)

    def stage_suffix(st):
        st["uptos"] = [sb_suffix(z) for z in st["zs"]]

    def stage_values(sb, st):
        lowest = None
        for p in sb_pairs:
            f = sb_accumulate(sb, p, st["zs"][p], st["uptos"][p], st["kp"], first=True)
            lowest = f if lowest is None else jnp.minimum(lowest, f)
        lowest = jnp.where(st["blk"] >= 2, lowest, jnp.inf)
        lowest_ref[sb] = jnp.min(lowest.reshape(-1, SUBLANES, LANES), axis=0)

    out_chunks = list(range(0, o_ref.shape[2], CHUNK))

    def next_out_chunk():
        if out_chunks:
            c = out_chunks.pop(0)
            o_ref[0, :, c : c + CHUNK] = x_ref[0, :, c : c + CHUNK] + jnp.dot(
                ya_ref[0], wout_ref[:SWA_WIDTH, c : c + CHUNK], preferred_element_type=jnp.float32
            )

    stages = [None] * n_sub
    stages[0] = stage_scores(0)
    for sb in range(n_sub):
        next_out_chunk()
        stage_suffix(stages[sb])
        if sb + 1 < n_sub:
            stages[sb + 1] = stage_scores(sb + 1)
        stage_values(sb, stages[sb])
        stages[sb] = None
    while out_chunks:
        next_out_chunk()

    @pl.when(jnp.min(lowest_ref[...]) < SB_DONE_LOG2)
    def _():
        def earlier_tiles(sb, carry):
            blk = block_index(sb)
            q2s = sb_queries(pl.multiple_of(sb * BLOCK, BLOCK))
            n_tiles = blk // 2

            def cond(c):
                n, lowest = c
                return jnp.logical_and(n < n_tiles, lowest < SB_DONE_LOG2)

            def body(c):
                n, _ = c
                start = (blk - 1) * BLOCK - CHUNK * (n + 1)
                ks = pl.multiple_of(jnp.maximum(start, 0), BLOCK)
                tile_valid = key_w < (CHUNK + jnp.minimum(start, 0))
                zs = [sb_scores(p, q2s[p], ks, tile_valid) for p in sb_pairs]
                uptos = [sb_suffix(z) for z in zs]
                lowest = None
                for p in sb_pairs:
                    f = sb_accumulate(sb, p, zs[p], uptos[p], ks, first=False)
                    lowest = f if lowest is None else jnp.minimum(lowest, f)
                return n + 1, jnp.min(lowest)

            lax.while_loop(cond, body, (jnp.int32(0), jnp.min(lowest_ref[sb])))
            return carry

        lax.fori_loop(0, n_sub, earlier_tiles, 0)

    for sb in range(n_sub):
        rows = slice(sb * BLOCK, (sb + 1) * BLOCK)
        for p in sb_pairs:
            gate = gb_ref[0, rows, p * LANES : (p + 1) * LANES].astype(jnp.float32)
            yb_ref[rows, p * LANES : (p + 1) * LANES] = (unstack_heads(acc_ref[sb, p]) * gate).astype(
                yb_ref.dtype
            )
    o_ref[0] += jnp.dot(yb_ref[...], wout_ref[SWA_WIDTH:, :], preferred_element_type=jnp.float32)


def _mixer(x, act, wout_bf16, tq):
    b, s, d = x.shape
    blk = lambda slot: pl.BlockSpec((1, tq, SLOT), lambda bi, i, slot=slot: (bi, i, slot))
    seq = lambda slot: pl.BlockSpec((1, s, SLOT), lambda bi, i, slot=slot: (bi, 0, slot))
    return pl.pallas_call(
        functools.partial(_mixer_kernel, tq=tq),
        out_shape=jax.ShapeDtypeStruct((b, s, d), jnp.float32),
        grid=(b, s // tq),
        in_specs=[
            pl.BlockSpec((1, tq, d), lambda bi, i: (bi, i, 0)),
            blk(SLOT_YA), blk(SLOT_QB), blk(SLOT_GB),
            seq(SLOT_KB), seq(SLOT_VB),
            pl.BlockSpec(wout_bf16.shape, lambda bi, i: (0, 0)),
        ],
        out_specs=pl.BlockSpec((1, tq, d), lambda bi, i: (bi, i, 0)),
        scratch_shapes=[
            pltpu.VMEM((tq, SB_WIDTH), jnp.bfloat16),
            pltpu.VMEM((tq // BLOCK, SB_PAIRS, 2 * BLOCK, LANES), jnp.float32),
            pltpu.VMEM((tq // BLOCK, SB_PAIRS, 2 * BLOCK, LANES), jnp.float32),
            pltpu.VMEM((tq // BLOCK, SUBLANES, LANES), jnp.float32),
        ],
        compiler_params=pltpu.CompilerParams(
            dimension_semantics=("arbitrary", "arbitrary"), vmem_limit_bytes=VMEM_LIMIT_BYTES
        ),
        name="mixer",
    )(x, act, act, act, act, act, wout_bf16)


def kernel(x, positions, norm_gain, w_in, q_norm_gain, k_norm_gain, sinks, w_out):
    b, s, d = x.shape
    tile = 512
    assert s % tile == 0 and s >= CHUNK
    depth = w_in.shape[0]
    half = HEAD_DIM // 2
    inv_freq = ROPE_THETA ** (-jnp.arange(half, dtype=jnp.float32) * 2.0 / HEAD_DIM)
    invf = jnp.tile(inv_freq, LANES // half).reshape(1, LANES)
    pos2 = positions.reshape(b * s, 1)
    for l in range(depth):
        act = _inproj(
            sinks[l].reshape(1, SWA_Q_HEADS),
            x.reshape(b * s, d),
            pos2,
            norm_gain[l].reshape(1, d),
            w_in[l].astype(jnp.bfloat16),
            jnp.tile(q_norm_gain[l], LANES // HEAD_DIM).reshape(1, LANES),
            jnp.tile(k_norm_gain[l], LANES // HEAD_DIM).reshape(1, LANES),
            invf,
            tm=tile,
            seq_len=s,
        )
        x = _mixer(x, act.reshape(b, s, ACT_WIDTH), w_out[l].astype(jnp.bfloat16), tq=tile)
    return x
```

```python
import functools
import math

import jax
import jax.numpy as jnp
from jax import lax
from jax.experimental import pallas as pl
from jax.experimental.pallas import tpu as pltpu

HEAD_DIM = 64
SWA_Q_HEADS = 8
SWA_KV_HEADS = 2
SB_HEADS = 8
BLOCK = 128
ROPE_THETA = 10000.0
EPS = 1e-6
LANES = 128
SUBLANES = 8
CHUNK = 256
LOG2E = math.log2(math.e)

SWA_WIDTH = SWA_Q_HEADS * HEAD_DIM
SWA_KV_WIDTH = SWA_KV_HEADS * HEAD_DIM
SB_WIDTH = SB_HEADS * HEAD_DIM
MIX_WIDTH = SWA_WIDTH + SB_WIDTH
SWA_PAIRS = SWA_Q_HEADS // 2
SB_PAIRS = SB_HEADS // 2

SRC_QA = 0
SRC_KA = SRC_QA + SWA_WIDTH
SRC_VA = SRC_KA + SWA_KV_WIDTH
SRC_GA = SRC_VA + SWA_KV_WIDTH
SRC_QB = SRC_GA + SWA_WIDTH
SRC_KB = SRC_QB + SB_WIDTH
SRC_VB = SRC_KB + SB_WIDTH
SRC_GB = SRC_VB + SB_WIDTH

SLOT = 512
SLOT_YA, SLOT_QB, SLOT_KB, SLOT_VB, SLOT_GB = range(5)
ACT_WIDTH = 5 * SLOT

VMEM_LIMIT_BYTES = 56 * 1024 * 1024

SB_DONE_LOG2 = 128.0
MASKED_SCORE = -1e30

_NT = (((1,), (1,)), ((), ()))


def _lane_iota(shape):
    return lax.broadcasted_iota(jnp.int32, shape, len(shape) - 1)


def _split_bf16(v):
    hi = v.astype(jnp.bfloat16)
    lo = (v - hi.astype(jnp.float32)).astype(jnp.bfloat16)
    return hi, lo


def _dot_split(v, rhs):
    hi, lo = _split_bf16(v)
    return jnp.dot(hi, rhs, preferred_element_type=jnp.float32) + jnp.dot(
        lo, rhs, preferred_element_type=jnp.float32
    )


def _silu(g):
    return g * (1.0 / (1.0 + jnp.exp(-g)))


def _softplus_log2(z2):
    return jnp.maximum(z2, 0.0) + jnp.log2(1.0 + jnp.exp2(-jnp.abs(z2)))


def _pair_masks():
    row2 = lax.broadcasted_iota(jnp.int32, (2 * BLOCK, LANES), 0)
    lane2 = lax.broadcasted_iota(jnp.int32, (2 * BLOCK, LANES), 1)
    own_head = (row2 < BLOCK) == (lane2 < HEAD_DIM)
    low = _lane_iota((1, LANES)) < HEAD_DIM

    def stack_heads(q_pair):
        q2 = jnp.concatenate([q_pair, q_pair], axis=0)
        return jnp.where(own_head, q2, jnp.zeros_like(q2))

    def unstack_heads(o2):
        return jnp.where(low, o2[:BLOCK], o2[BLOCK:])

    return row2, lane2, stack_heads, unstack_heads


def _inproj_kernel(
    sinks_ref, x_ref, pos_ref, gain_ref, w_ref, qgain_ref, kgain_ref, invf_ref,
    o_ref, qa_ref, ga_ref, kv_ref, *, tiles_per_seq,
):
    tm = x_ref.shape[0]
    first_tile = (pl.program_id(0) % tiles_per_seq) == 0

    @pl.when(first_tile)
    def _():
        kv_ref[0:BLOCK, :] = jnp.zeros((BLOCK, kv_ref.shape[1]), kv_ref.dtype)

    x = x_ref[...]
    ms = jnp.mean(x * x, axis=-1, keepdims=True)
    h = (x * lax.rsqrt(ms + EPS) * gain_ref[...]).astype(jnp.bfloat16)

    lane = _lane_iota((1, LANES))
    first_half = (lane % HEAD_DIM) < (HEAD_DIM // 2)
    n_groups = LANES // (HEAD_DIM // 2)
    quarter = tm // n_groups
    group = lane // (HEAD_DIM // 2)
    pos = pos_ref[...].astype(jnp.float32)
    packed_pos = pos[(n_groups - 1) * quarter :]
    for g in reversed(range(n_groups - 1)):
        packed_pos = jnp.where(group == g, pos[g * quarter : (g + 1) * quarter], packed_pos)
    packed_ang = packed_pos * invf_ref[...]

    def spread(packed):
        rolled = [packed] + [pltpu.roll(packed, s * (HEAD_DIM // 2), 1) for s in range(1, n_groups)]
        quarters = []
        for g in range(n_groups):
            t = rolled[(n_groups - 1 - g) % n_groups]
            for j in reversed(range(n_groups - 1)):
                t = jnp.where(group == j, rolled[(j - g) % n_groups], t)
            quarters.append(t)
        return jnp.concatenate(quarters, axis=0)

    cos = spread(jnp.cos(packed_ang))
    sin_signed = spread(jnp.sin(packed_ang)) * jnp.where(first_half, -1.0, 1.0)

    r = lax.broadcasted_iota(jnp.int32, (LANES, LANES), 0) // HEAD_DIM
    c = lax.broadcasted_iota(jnp.int32, (LANES, LANES), 1) // HEAD_DIM
    head_ones = jnp.where(r == c, 1.0, 0.0).astype(jnp.bfloat16)

    def norm_rope(a, head_gain, scale):
        ss = _dot_split(a * a, head_ones)
        y = a * lax.rsqrt(ss * (1.0 / HEAD_DIM) + EPS) * head_gain
        partner = jnp.where(
            first_half, pltpu.roll(y, LANES - HEAD_DIM // 2, 1), pltpu.roll(y, HEAD_DIM // 2, 1)
        )
        y = y * cos + partner * sin_signed
        return y * scale if scale != 1.0 else y

    def project(src_col):
        return jnp.dot(h, w_ref[:, src_col : src_col + CHUNK], preferred_element_type=jnp.float32)

    def store(slot, col, val):
        o_ref[:, slot * SLOT + col : slot * SLOT + col + val.shape[1]] = val.astype(o_ref.dtype)

    q_scale = LOG2E / math.sqrt(HEAD_DIM)

    def swa_queries(acc, col):
        for half in range(CHUNK // LANES):
            a = acc[:, half * LANES : (half + 1) * LANES]
            lanes = slice(col + half * LANES, col + (half + 1) * LANES)
            qa_ref[:, lanes] = norm_rope(a, qgain_ref[...], q_scale).astype(qa_ref.dtype)

    def swa_keys_values(acc):
        k = norm_rope(acc[:, :LANES], kgain_ref[...], 1.0)
        v = acc[:, LANES:]
        low = lane < HEAD_DIM
        for base, t in ((0, k), (2 * LANES, v)):
            swapped = pltpu.roll(t, HEAD_DIM, 1)
            kv_ref[BLOCK:, base : base + LANES] = jnp.where(low, t, swapped).astype(kv_ref.dtype)
            kv_ref[BLOCK:, base + LANES : base + 2 * LANES] = jnp.where(low, swapped, t).astype(kv_ref.dtype)

    row2, lane2, stack_heads, unstack_heads = _pair_masks()
    in_cur_block = lane2 <= (row2 & (BLOCK - 1))
    kv_group = [(2 * p) // (SWA_Q_HEADS // SWA_KV_HEADS) for p in range(SWA_PAIRS)]

    def swa_scores(j):
        scores = []
        for p in range(SWA_PAIRS):
            q2 = stack_heads(qa_ref[j * BLOCK : (j + 1) * BLOCK, p * LANES : (p + 1) * LANES])
            k_win = kv_ref[j * BLOCK : (j + 2) * BLOCK, kv_group[p] * LANES : (kv_group[p] + 1) * LANES]
            scores.append(lax.dot_general(q2, k_win, _NT, preferred_element_type=jnp.float32))
        return scores

    def swa_softmax(j, scores):
        probs = []
        for p in range(SWA_PAIRS):
            s = jnp.where(in_cur_block, scores[p][:, BLOCK:], scores[p][:, :BLOCK])
            if j == 0:
                has_prev_cells = jnp.logical_not((jnp.zeros_like(lane2) + first_tile.astype(jnp.int32)) > 0)
                s = jnp.where(jnp.logical_or(in_cur_block, has_prev_cells), s, -jnp.inf)
            sink = jnp.where(
                row2[:, :1] < BLOCK, LOG2E * sinks_ref[0, 2 * p], LOG2E * sinks_ref[0, 2 * p + 1]
            )
            m = jnp.max(s, axis=-1, keepdims=True)
            e = jnp.exp2(s - m)
            denom = jnp.sum(e, axis=-1, keepdims=True) + jnp.exp2(sink - m)
            e_both = jnp.concatenate(
                [jnp.where(in_cur_block, 0.0, e), jnp.where(in_cur_block, e, 0.0)], axis=1
            ).astype(jnp.bfloat16)
            probs.append((e_both, denom))
        return probs

    def swa_values(j, probs):
        rows = slice(j * BLOCK, (j + 1) * BLOCK)
        for p in range(SWA_PAIRS):
            e_both, denom = probs[p]
            v_cols = slice((SWA_KV_HEADS + kv_group[p]) * LANES, (SWA_KV_HEADS + kv_group[p] + 1) * LANES)
            o2 = jnp.dot(e_both, kv_ref[j * BLOCK : (j + 2) * BLOCK, v_cols], preferred_element_type=jnp.float32)
            gate = ga_ref[rows, p * LANES : (p + 1) * LANES].astype(jnp.float32)
            o_ref[rows, SLOT_YA * SLOT + p * LANES : SLOT_YA * SLOT + (p + 1) * LANES] = (
                unstack_heads(o2 / denom) * gate
            ).astype(o_ref.dtype)

    first, second = 0, CHUNK
    qa_first = project(SRC_QA + first)
    kva = project(SRC_KA)
    qa_second = project(SRC_QA + second)
    ga_ref[:, first : first + CHUNK] = _silu(project(SRC_GA + first)).astype(ga_ref.dtype)
    swa_queries(qa_first, first)
    ga_ref[:, second : second + CHUNK] = _silu(project(SRC_GA + second)).astype(ga_ref.dtype)
    swa_keys_values(kva)
    store(SLOT_QB, first, project(SRC_QB + first) * q_scale)
    swa_queries(qa_second, second)
    store(SLOT_QB, second, project(SRC_QB + second) * q_scale)

    plain_chunks = [
        (SLOT_KB, SRC_KB, first, None), (SLOT_KB, SRC_KB, second, None),
        (SLOT_VB, SRC_VB, first, None), (SLOT_VB, SRC_VB, second, None),
        (SLOT_GB, SRC_GB, first, _silu), (SLOT_GB, SRC_GB, second, _silu),
    ]

    def next_plain_chunk():
        if plain_chunks:
            slot, src, col, post = plain_chunks.pop(0)
            val = project(src + col)
            store(slot, col, post(val) if post else val)

    n_blk = tm // BLOCK
    scores = swa_scores(0)
    for j in range(n_blk):
        next_plain_chunk()
        probs = swa_softmax(j, scores)
        if j + 1 < n_blk:
            scores = swa_scores(j + 1)
        next_plain_chunk()
        swa_values(j, probs)
    while plain_chunks:
        next_plain_chunk()

    kv_ref[0:BLOCK, :] = kv_ref[tm : tm + BLOCK, :]


def _inproj(sinks, x2, pos2, gain, w_bf16, qgain, kgain, invf, tm, seq_len):
    n, d = x2.shape
    full = lambda i: (0, 0)
    return pl.pallas_call(
        functools.partial(_inproj_kernel, tiles_per_seq=seq_len // tm),
        out_shape=jax.ShapeDtypeStruct((n, ACT_WIDTH), jnp.bfloat16),
        grid=(n // tm,),
        in_specs=[
            pl.BlockSpec(memory_space=pltpu.SMEM),
            pl.BlockSpec((tm, d), lambda i: (i, 0)),
            pl.BlockSpec((tm, 1), lambda i: (i, 0)),
            pl.BlockSpec((1, d), full),
            pl.BlockSpec(w_bf16.shape, full),
            pl.BlockSpec((1, LANES), full),
            pl.BlockSpec((1, LANES), full),
            pl.BlockSpec((1, LANES), full),
        ],
        out_specs=pl.BlockSpec((tm, ACT_WIDTH), lambda i: (i, 0)),
        scratch_shapes=[
            pltpu.VMEM((tm, SWA_WIDTH), jnp.bfloat16),
            pltpu.VMEM((tm, SWA_WIDTH), jnp.bfloat16),
            pltpu.VMEM((BLOCK + tm, 2 * SWA_KV_HEADS * LANES), jnp.bfloat16),
        ],
        compiler_params=pltpu.CompilerParams(
            dimension_semantics=("arbitrary",), vmem_limit_bytes=VMEM_LIMIT_BYTES
        ),
        name="inproj",
    )(sinks, x2, pos2, gain, w_bf16, qgain, kgain, invf)


def _mixer_kernel(
    x_ref, ya_ref, qb_ref, gb_ref, kb_ref, vb_ref, wout_ref,
    o_ref, yb_ref, acc_ref, fail_ref, lowest_ref, *, tq,
):
    step = pl.program_id(1)
    n_sub = tq // BLOCK
    _, _, stack_heads, unstack_heads = _pair_masks()
    qrow_w = lax.broadcasted_iota(jnp.int32, (2 * BLOCK, CHUNK), 0) & (BLOCK - 1)
    key_w = lax.broadcasted_iota(jnp.int32, (2 * BLOCK, CHUNK), 1)
    kr = lax.broadcasted_iota(jnp.int32, (CHUNK, CHUNK), 0)
    kc = lax.broadcasted_iota(jnp.int32, (CHUNK, CHUNK), 1)
    suffix_ones = jnp.where(kr >= kc, 1.0, 0.0).astype(jnp.bfloat16)
    sb_pairs = range(SB_PAIRS)

    def sb_scores(p, q2, ks, valid):
        k_t = kb_ref[0, pl.ds(ks, CHUNK), p * LANES : (p + 1) * LANES]
        z = lax.dot_general(q2, k_t, _NT, preferred_element_type=jnp.float32)
        return jnp.where(valid, z, MASKED_SCORE)

    def sb_suffix(z):
        return jnp.dot(
            _softplus_log2(z).astype(jnp.bfloat16), suffix_ones, preferred_element_type=jnp.float32
        )

    def sb_accumulate(sb, p, z, upto, ks, first):
        arg = z - upto
        if not first:
            fail = fail_ref[sb, p]
            arg = arg - jnp.concatenate([fail] * (CHUNK // LANES), axis=1)
        w = jnp.exp2(arg).astype(jnp.bfloat16)
        v_t = vb_ref[0, pl.ds(ks, CHUNK), p * LANES : (p + 1) * LANES]
        pv = jnp.dot(w, v_t, preferred_element_type=jnp.float32)
        total = jnp.broadcast_to(upto[:, :1], (2 * BLOCK, LANES))
        if first:
            acc_ref[sb, p] = pv
            fail_ref[sb, p] = total
            return total
        acc_ref[sb, p] += pv
        fail_ref[sb, p] = fail + total
        return fail + total

    def block_index(sb):
        return step * n_sub + sb

    def sb_queries(r0):
        return [stack_heads(qb_ref[0, pl.ds(r0, BLOCK), p * LANES : (p + 1) * LANES]) for p in sb_pairs]

    def stage_scores(sb):
        blk = block_index(sb)
        kp = pl.multiple_of(jnp.maximum(blk - 1, 0) * BLOCK, BLOCK)
        window_valid = key_w < (qrow_w + jnp.where(blk > 0, BLOCK, 0))
        zs = [sb_scores(p, q2, kp, window_valid) for p, q2 in enumerate(sb_queries(sb * BLOCK))]
        return dict(blk=blk, kp=kp, zs=zs)

    def stage_suffix(st):
        st["uptos"] = [sb_suffix(z) for z in st["zs"]]

    def stage_values(sb, st):
        lowest = None
        for p in sb_pairs:
            f = sb_accumulate(sb, p, st["zs"][p], st["uptos"][p], st["kp"], first=True)
            lowest = f if lowest is None else jnp.minimum(lowest, f)
        lowest = jnp.where(st["blk"] >= 2, lowest, jnp.inf)
        lowest_ref[sb] = jnp.min(lowest.reshape(-1, SUBLANES, LANES), axis=0)

    out_chunks = list(range(0, o_ref.shape[2], CHUNK))

    def next_out_chunk():
        if out_chunks:
            c = out_chunks.pop(0)
            o_ref[0, :, c : c + CHUNK] = x_ref[0, :, c : c + CHUNK] + jnp.dot(
                ya_ref[0], wout_ref[:SWA_WIDTH, c : c + CHUNK], preferred_element_type=jnp.float32
            )

    stages = [None] * n_sub
    stages[0] = stage_scores(0)
    for sb in range(n_sub):
        next_out_chunk()
        stage_suffix(stages[sb])
        if sb + 1 < n_sub:
            stages[sb + 1] = stage_scores(sb + 1)
        stage_values(sb, stages[sb])
        stages[sb] = None
    while out_chunks:
        next_out_chunk()

    @pl.when(jnp.min(lowest_ref[...]) < SB_DONE_LOG2)
    def _():
        def earlier_tiles(sb, carry):
            blk = block_index(sb)
            q2s = sb_queries(pl.multiple_of(sb * BLOCK, BLOCK))
            n_tiles = blk // 2

            def cond(c):
                n, lowest = c
                return jnp.logical_and(n < n_tiles, lowest < SB_DONE_LOG2)

            def body(c):
                n, _ = c
                start = (blk - 1) * BLOCK - CHUNK * (n + 1)
                ks = pl.multiple_of(jnp.maximum(start, 0), BLOCK)
                tile_valid = key_w < (CHUNK + jnp.minimum(start, 0))
                zs = [sb_scores(p, q2s[p], ks, tile_valid) for p in sb_pairs]
                uptos = [sb_suffix(z) for z in zs]
                lowest = None
                for p in sb_pairs:
                    f = sb_accumulate(sb, p, zs[p], uptos[p], ks, first=False)
                    lowest = f if lowest is None else jnp.minimum(lowest, f)
                return n + 1, jnp.min(lowest)

            lax.while_loop(cond, body, (jnp.int32(0), jnp.min(lowest_ref[sb])))
            return carry

        lax.fori_loop(0, n_sub, earlier_tiles, 0)

    for sb in range(n_sub):
        rows = slice(sb * BLOCK, (sb + 1) * BLOCK)
        for p in sb_pairs:
            gate = gb_ref[0, rows, p * LANES : (p + 1) * LANES].astype(jnp.float32)
            yb_ref[rows, p * LANES : (p + 1) * LANES] = (unstack_heads(acc_ref[sb, p]) * gate).astype(
                yb_ref.dtype
            )
    o_ref[0] += jnp.dot(yb_ref[...], wout_ref[SWA_WIDTH:, :], preferred_element_type=jnp.float32)


def _mixer(x, act, wout_bf16, tq):
    b, s, d = x.shape
    blk = lambda slot: pl.BlockSpec((1, tq, SLOT), lambda bi, i, slot=slot: (bi, i, slot))
    seq = lambda slot: pl.BlockSpec((1, s, SLOT), lambda bi, i, slot=slot: (bi, 0, slot))
    return pl.pallas_call(
        functools.partial(_mixer_kernel, tq=tq),
        out_shape=jax.ShapeDtypeStruct((b, s, d), jnp.float32),
        grid=(b, s // tq),
        in_specs=[
            pl.BlockSpec((1, tq, d), lambda bi, i: (bi, i, 0)),
            blk(SLOT_YA), blk(SLOT_QB), blk(SLOT_GB),
            seq(SLOT_KB), seq(SLOT_VB),
            pl.BlockSpec(wout_bf16.shape, lambda bi, i: (0, 0)),
        ],
        out_specs=pl.BlockSpec((1, tq, d), lambda bi, i: (bi, i, 0)),
        scratch_shapes=[
            pltpu.VMEM((tq, SB_WIDTH), jnp.bfloat16),
            pltpu.VMEM((tq // BLOCK, SB_PAIRS, 2 * BLOCK, LANES), jnp.float32),
            pltpu.VMEM((tq // BLOCK, SB_PAIRS, 2 * BLOCK, LANES), jnp.float32),
            pltpu.VMEM((tq // BLOCK, SUBLANES, LANES), jnp.float32),
        ],
        compiler_params=pltpu.CompilerParams(
            dimension_semantics=("arbitrary", "arbitrary"), vmem_limit_bytes=VMEM_LIMIT_BYTES
        ),
        name="mixer",
    )(x, act, act, act, act, act, wout_bf16)


def kernel(x, positions, norm_gain, w_in, q_norm_gain, k_norm_gain, sinks, w_out):
    b, s, d = x.shape
    tile = 1024
    assert s % tile == 0 and s >= CHUNK
    depth = w_in.shape[0]
    half = HEAD_DIM // 2
    inv_freq = ROPE_THETA ** (-jnp.arange(half, dtype=jnp.float32) * 2.0 / HEAD_DIM)
    invf = jnp.tile(inv_freq, LANES // half).reshape(1, LANES)
    pos2 = positions.reshape(b * s, 1)
    for l in range(depth):
        act = _inproj(
            sinks[l].reshape(1, SWA_Q_HEADS),
            x.reshape(b * s, d),
            pos2,
            norm_gain[l].reshape(1, d),
            w_in[l].astype(jnp.bfloat16),
            jnp.tile(q_norm_gain[l], LANES // HEAD_DIM).reshape(1, LANES),
            jnp.tile(k_norm_gain[l], LANES // HEAD_DIM).reshape(1, LANES),
            invf,
            tm=tile,
            seq_len=s,
        )
        x = _mixer(x, act.reshape(b, s, ACT_WIDTH), w_out[l].astype(jnp.bfloat16), tq=tile)
    return x
```

```python
import functools
import math

import jax
import jax.numpy as jnp
from jax import lax
from jax.experimental import pallas as pl
from jax.experimental.pallas import tpu as pltpu

HEAD_DIM = 64
SWA_Q_HEADS = 8
SWA_KV_HEADS = 2
SB_HEADS = 8
BLOCK = 128
ROPE_THETA = 10000.0
EPS = 1e-6
LANES = 128
SUBLANES = 8
CHUNK = 256
LOG2E = math.log2(math.e)

SWA_WIDTH = SWA_Q_HEADS * HEAD_DIM
SWA_KV_WIDTH = SWA_KV_HEADS * HEAD_DIM
SB_WIDTH = SB_HEADS * HEAD_DIM
MIX_WIDTH = SWA_WIDTH + SB_WIDTH
SWA_PAIRS = SWA_Q_HEADS // 2
SB_PAIRS = SB_HEADS // 2

SRC_QA = 0
SRC_KA = SRC_QA + SWA_WIDTH
SRC_VA = SRC_KA + SWA_KV_WIDTH
SRC_GA = SRC_VA + SWA_KV_WIDTH
SRC_QB = SRC_GA + SWA_WIDTH
SRC_KB = SRC_QB + SB_WIDTH
SRC_VB = SRC_KB + SB_WIDTH
SRC_GB = SRC_VB + SB_WIDTH

SLOT = 512
SLOT_YA, SLOT_QB, SLOT_KB, SLOT_VB, SLOT_GB = range(5)
ACT_WIDTH = 5 * SLOT

VMEM_LIMIT_BYTES = 56 * 1024 * 1024

SB_DONE_LOG2 = 128.0
MASKED_SCORE = -1e30

_NT = (((1,), (1,)), ((), ()))


def _lane_iota(shape):
    return lax.broadcasted_iota(jnp.int32, shape, len(shape) - 1)


def _rmsnorm_bf16(x, gain):
    ms = jnp.mean(x * x, axis=-1, keepdims=True)
    return (x * lax.rsqrt(ms + EPS) * gain).astype(jnp.bfloat16)


def _silu(g):
    return g * (1.0 / (1.0 + jnp.exp(-g)))


def _softplus_log2(z2):
    return jnp.maximum(z2, 0.0) + jnp.log2(1.0 + jnp.exp2(-jnp.abs(z2)))


def _pair_masks():
    row2 = lax.broadcasted_iota(jnp.int32, (2 * BLOCK, LANES), 0)
    lane2 = lax.broadcasted_iota(jnp.int32, (2 * BLOCK, LANES), 1)
    own_head = (row2 < BLOCK) == (lane2 < HEAD_DIM)
    low = _lane_iota((1, LANES)) < HEAD_DIM

    def stack_heads(q_pair):
        q2 = jnp.concatenate([q_pair, q_pair], axis=0)
        return jnp.where(own_head, q2, jnp.zeros_like(q2))

    def unstack_heads(o2):
        return jnp.where(low, o2[:BLOCK], o2[BLOCK:])

    return row2, lane2, stack_heads, unstack_heads


def _inproj_kernel(
    sinks_ref, x_ref, pos_ref, gain_ref, w_ref, qgain_ref, kgain_ref, invf_ref,
    o_ref, qa_ref, ga_ref, kv_ref, *, tiles_per_seq,
):
    tm = x_ref.shape[0]
    first_tile = (pl.program_id(0) % tiles_per_seq) == 0

    @pl.when(first_tile)
    def _():
        kv_ref[0:BLOCK, :] = jnp.zeros((BLOCK, kv_ref.shape[1]), kv_ref.dtype)

    h = _rmsnorm_bf16(x_ref[...], gain_ref[...])

    lane = _lane_iota((1, LANES))
    first_half = (lane % HEAD_DIM) < (HEAD_DIM // 2)
    low = lane < HEAD_DIM
    n_groups = LANES // (HEAD_DIM // 2)
    group = lane // (HEAD_DIM // 2)
    packed_ang = pos_ref[...].astype(jnp.float32) * invf_ref[...]

    def spread(packed):
        rolled = [packed] + [pltpu.roll(packed, s * (HEAD_DIM // 2), 1) for s in range(1, n_groups)]
        quarters = []
        for g in range(n_groups):
            t = rolled[(n_groups - 1 - g) % n_groups]
            for j in reversed(range(n_groups - 1)):
                t = jnp.where(group == j, rolled[(j - g) % n_groups], t)
            quarters.append(t)
        return jnp.concatenate(quarters, axis=0)

    cos = spread(jnp.cos(packed_ang))
    sin_signed = spread(jnp.sin(packed_ang)) * jnp.where(first_half, -1.0, 1.0)

    r = lax.broadcasted_iota(jnp.int32, (LANES, LANES), 0) // HEAD_DIM
    c = lax.broadcasted_iota(jnp.int32, (LANES, LANES), 1) // HEAD_DIM
    head_ones = jnp.where(r == c, 1.0, 0.0).astype(jnp.bfloat16)

    def norm_rope(a, head_gain, scale):
        ss = jnp.dot((a * a).astype(jnp.bfloat16), head_ones, preferred_element_type=jnp.float32)
        y = a * lax.rsqrt(ss * (1.0 / HEAD_DIM) + EPS) * head_gain
        partner = jnp.where(
            first_half, pltpu.roll(y, LANES - HEAD_DIM // 2, 1), pltpu.roll(y, HEAD_DIM // 2, 1)
        )
        y = y * cos + partner * sin_signed
        return y * scale if scale != 1.0 else y

    def project(src_col):
        return jnp.dot(h, w_ref[:, src_col : src_col + CHUNK], preferred_element_type=jnp.float32)

    def store(slot, col, val):
        o_ref[:, slot * SLOT + col : slot * SLOT + col + val.shape[1]] = val.astype(o_ref.dtype)

    q_scale = LOG2E / math.sqrt(HEAD_DIM)

    def swa_queries(acc, col):
        for half in range(CHUNK // LANES):
            a = acc[:, half * LANES : (half + 1) * LANES]
            lanes = slice(col + half * LANES, col + (half + 1) * LANES)
            qa_ref[:, lanes] = norm_rope(a, qgain_ref[...], q_scale).astype(qa_ref.dtype)

    def swa_keys_values(acc):
        k = norm_rope(acc[:, :LANES], kgain_ref[...], 1.0)
        v = acc[:, LANES:]
        for base, t in ((0, k), (2 * LANES, v)):
            swapped = pltpu.roll(t, HEAD_DIM, 1)
            kv_ref[BLOCK:, base : base + LANES] = jnp.where(low, t, swapped).astype(kv_ref.dtype)
            kv_ref[BLOCK:, base + LANES : base + 2 * LANES] = jnp.where(low, swapped, t).astype(kv_ref.dtype)

    row2, lane2, stack_heads, unstack_heads = _pair_masks()
    in_cur_block = lane2 <= (row2 & (BLOCK - 1))
    kv_group = [(2 * p) // (SWA_Q_HEADS // SWA_KV_HEADS) for p in range(SWA_PAIRS)]

    def swa_scores(j):
        scores = []
        for p in range(SWA_PAIRS):
            q2 = stack_heads(qa_ref[j * BLOCK : (j + 1) * BLOCK, p * LANES : (p + 1) * LANES])
            k_win = kv_ref[j * BLOCK : (j + 2) * BLOCK, kv_group[p] * LANES : (kv_group[p] + 1) * LANES]
            scores.append(lax.dot_general(q2, k_win, _NT, preferred_element_type=jnp.float32))
        return scores

    def swa_softmax(j, scores):
        probs = []
        for p in range(SWA_PAIRS):
            s = jnp.where(in_cur_block, scores[p][:, BLOCK:], scores[p][:, :BLOCK])
            if j == 0:
                has_prev_cells = jnp.logical_not((jnp.zeros_like(lane2) + first_tile.astype(jnp.int32)) > 0)
                s = jnp.where(jnp.logical_or(in_cur_block, has_prev_cells), s, -jnp.inf)
            sink = jnp.where(
                row2[:, :1] < BLOCK, LOG2E * sinks_ref[0, 2 * p], LOG2E * sinks_ref[0, 2 * p + 1]
            )
            m = jnp.max(s, axis=-1, keepdims=True)
            e = jnp.exp2(s - m)
            denom = jnp.sum(e, axis=-1, keepdims=True) + jnp.exp2(sink - m)
            e_both = jnp.concatenate(
                [jnp.where(in_cur_block, 0.0, e), jnp.where(in_cur_block, e, 0.0)], axis=1
            ).astype(jnp.bfloat16)
            probs.append((e_both, denom))
        return probs

    def swa_values(j, probs):
        rows = slice(j * BLOCK, (j + 1) * BLOCK)
        for p in range(SWA_PAIRS):
            e_both, denom = probs[p]
            v_cols = slice((SWA_KV_HEADS + kv_group[p]) * LANES, (SWA_KV_HEADS + kv_group[p] + 1) * LANES)
            o2 = jnp.dot(e_both, kv_ref[j * BLOCK : (j + 2) * BLOCK, v_cols], preferred_element_type=jnp.float32)
            gate = ga_ref[rows, p * LANES : (p + 1) * LANES].astype(jnp.float32)
            o_ref[rows, SLOT_YA * SLOT + p * LANES : SLOT_YA * SLOT + (p + 1) * LANES] = (
                unstack_heads(o2 / denom) * gate
            ).astype(o_ref.dtype)

    first, second = 0, CHUNK
    qa_first = project(SRC_QA + first)
    kva = project(SRC_KA)
    qa_second = project(SRC_QA + second)
    ga_ref[:, first : first + CHUNK] = _silu(project(SRC_GA + first)).astype(ga_ref.dtype)
    swa_queries(qa_first, first)
    ga_ref[:, second : second + CHUNK] = _silu(project(SRC_GA + second)).astype(ga_ref.dtype)
    swa_keys_values(kva)
    half_rows = tm // 2
    scale_q = lambda v: v * q_scale
    plain_pieces = [
        (slot, src, col, r0, post)
        for slot, src, post in (
            (SLOT_QB, SRC_QB, scale_q), (SLOT_KB, SRC_KB, None), (SLOT_VB, SRC_VB, None), (SLOT_GB, SRC_GB, _silu)
        )
        for col in range(0, SLOT, CHUNK)
        for r0 in (0, half_rows)
    ]

    def plain_piece(slot, src, col, r0, post):
        val = jnp.dot(
            h[r0 : r0 + half_rows], w_ref[:, src + col : src + col + CHUNK], preferred_element_type=jnp.float32
        )
        val = post(val) if post else val
        o_ref[r0 : r0 + half_rows, slot * SLOT + col : slot * SLOT + col + CHUNK] = val.astype(o_ref.dtype)

    swa_queries(qa_second, second)
    n_blk = tm // BLOCK
    n_slots = 2 * n_blk
    by_slot = [[] for _ in range(n_slots)]
    for k, piece in enumerate(plain_pieces):
        by_slot[k * n_slots // len(plain_pieces)].append(piece)

    scores = swa_scores(0)
    for j in range(n_blk):
        for piece in by_slot[2 * j]:
            plain_piece(*piece)
        probs = swa_softmax(j, scores)
        if j + 1 < n_blk:
            scores = swa_scores(j + 1)
        for piece in by_slot[2 * j + 1]:
            plain_piece(*piece)
        swa_values(j, probs)

    kv_ref[0:BLOCK, :] = kv_ref[tm : tm + BLOCK, :]


def _inproj(sinks, x2, pos2, gain, w_bf16, qgain, kgain, invf, tm, seq_len):
    n, d = x2.shape
    full = lambda i: (0, 0)
    return pl.pallas_call(
        functools.partial(_inproj_kernel, tiles_per_seq=seq_len // tm),
        out_shape=jax.ShapeDtypeStruct((n, ACT_WIDTH), jnp.bfloat16),
        grid=(n // tm,),
        in_specs=[
            pl.BlockSpec(memory_space=pltpu.SMEM),
            pl.BlockSpec((tm, d), lambda i: (i, 0)),
            pl.BlockSpec((tm // (LANES // (HEAD_DIM // 2)), LANES), lambda i: (i, 0)),
            pl.BlockSpec((1, d), full),
            pl.BlockSpec(w_bf16.shape, full),
            pl.BlockSpec((1, LANES), full),
            pl.BlockSpec((1, LANES), full),
            pl.BlockSpec((1, LANES), full),
        ],
        out_specs=pl.BlockSpec((tm, ACT_WIDTH), lambda i: (i, 0)),
        scratch_shapes=[
            pltpu.VMEM((tm, SWA_WIDTH), jnp.bfloat16),
            pltpu.VMEM((tm, SWA_WIDTH), jnp.bfloat16),
            pltpu.VMEM((BLOCK + tm, 2 * SWA_KV_HEADS * LANES), jnp.bfloat16),
        ],
        compiler_params=pltpu.CompilerParams(
            dimension_semantics=("arbitrary",), vmem_limit_bytes=VMEM_LIMIT_BYTES
        ),
        name="inproj",
    )(sinks, x2, pos2, gain, w_bf16, qgain, kgain, invf)


def _mixer_kernel(
    x_ref, ya_ref, qb_ref, gb_ref, kb_ref, vb_ref, wout_ref,
    o_ref, yb_ref, acc_ref, fail_ref, lowest_ref, lowest_smem, *, tq,
):
    step = pl.program_id(1)
    n_sub = tq // BLOCK
    _, _, stack_heads, unstack_heads = _pair_masks()
    qrow_w = lax.broadcasted_iota(jnp.int32, (2 * BLOCK, CHUNK), 0) & (BLOCK - 1)
    key_w = lax.broadcasted_iota(jnp.int32, (2 * BLOCK, CHUNK), 1)
    kr = lax.broadcasted_iota(jnp.int32, (CHUNK, CHUNK), 0)
    kc = lax.broadcasted_iota(jnp.int32, (CHUNK, CHUNK), 1)
    suffix_ones = jnp.where(kr >= kc, 1.0, 0.0).astype(jnp.bfloat16)
    sb_pairs = range(SB_PAIRS)

    def sb_scores(p, q2, ks, valid):
        k_t = kb_ref[0, pl.ds(ks, CHUNK), p * LANES : (p + 1) * LANES]
        z = lax.dot_general(q2, k_t, _NT, preferred_element_type=jnp.float32)
        return jnp.where(valid, z, MASKED_SCORE)

    def sb_suffix(z):
        return jnp.dot(
            _softplus_log2(z).astype(jnp.bfloat16), suffix_ones, preferred_element_type=jnp.float32
        )

    def sb_accumulate(sb, p, z, upto, ks, first):
        arg = z - upto
        if not first:
            fail = fail_ref[sb, p]
            arg = arg - jnp.concatenate([fail] * (CHUNK // LANES), axis=1)
        w = jnp.exp2(arg).astype(jnp.bfloat16)
        v_t = vb_ref[0, pl.ds(ks, CHUNK), p * LANES : (p + 1) * LANES]
        pv = jnp.dot(w, v_t, preferred_element_type=jnp.float32)
        total = jnp.broadcast_to(upto[:, :1], (2 * BLOCK, LANES))
        if first:
            acc_ref[sb, p] = pv
            fail_ref[sb, p] = total
            return total
        acc_ref[sb, p] += pv
        fail_ref[sb, p] = fail + total
        return fail + total

    def block_index(sb):
        return step * n_sub + sb

    def sb_queries(r0):
        return [stack_heads(qb_ref[0, pl.ds(r0, BLOCK), p * LANES : (p + 1) * LANES]) for p in sb_pairs]

    def stage_scores(sb):
        blk = block_index(sb)
        kp = pl.multiple_of(jnp.maximum(blk - 1, 0) * BLOCK, BLOCK)
        window_valid = key_w < (qrow_w + jnp.where(blk > 0, BLOCK, 0))
        zs = [sb_scores(p, q2, kp, window_valid) for p, q2 in enumerate(sb_queries(sb * BLOCK))]
        return dict(blk=blk, kp=kp, zs=zs)

    def stage_suffix(st):
        st["uptos"] = [sb_suffix(z) for z in st["zs"]]

    def stage_values(sb, st):
        lowest = None
        for p in sb_pairs:
            f = sb_accumulate(sb, p, st["zs"][p], st["uptos"][p], st["kp"], first=True)
            lowest = f if lowest is None else jnp.minimum(lowest, f)
        lowest = jnp.where(st["blk"] >= 2, lowest, jnp.inf)
        lowest_ref[sb] = jnp.min(lowest.reshape(-1, SUBLANES, LANES), axis=0)

    out_chunks = list(range(0, o_ref.shape[2], CHUNK))

    def next_out_chunk():
        if out_chunks:
            c = out_chunks.pop(0)
            o_ref[0, :, c : c + CHUNK] = x_ref[0, :, c : c + CHUNK] + jnp.dot(
                ya_ref[0], wout_ref[:SWA_WIDTH, c : c + CHUNK], preferred_element_type=jnp.float32
            )

    stage = stage_scores(0)
    for sb in range(n_sub):
        next_out_chunk()
        stage_suffix(stage)
        following = stage_scores(sb + 1) if sb + 1 < n_sub else None
        stage_values(sb, stage)
        stage = following
    while out_chunks:
        next_out_chunk()

    @pl.when(jnp.min(lowest_ref[...]) < SB_DONE_LOG2)
    def _():
        for sb in range(n_sub):
            lowest_smem[sb] = jnp.min(lowest_ref[sb])

        def earlier_tiles(sb, carry):
            blk = block_index(sb)
            q2s = sb_queries(pl.multiple_of(sb * BLOCK, BLOCK))
            n_tiles = blk // 2

            def cond(c):
                n, lowest = c
                return jnp.logical_and(n < n_tiles, lowest < SB_DONE_LOG2)

            def body(c):
                n, _ = c
                start = (blk - 1) * BLOCK - CHUNK * (n + 1)
                ks = pl.multiple_of(jnp.maximum(start, 0), BLOCK)
                tile_valid = key_w < (CHUNK + jnp.minimum(start, 0))
                zs = [sb_scores(p, q2s[p], ks, tile_valid) for p in sb_pairs]
                uptos = [sb_suffix(z) for z in zs]
                lowest = None
                for p in sb_pairs:
                    f = sb_accumulate(sb, p, zs[p], uptos[p], ks, first=False)
                    lowest = f if lowest is None else jnp.minimum(lowest, f)
                return n + 1, jnp.min(lowest)

            lax.while_loop(cond, body, (jnp.int32(0), lowest_smem[sb]))
            return carry

        lax.fori_loop(0, n_sub, earlier_tiles, 0)

    for sb in range(n_sub):
        rows = slice(sb * BLOCK, (sb + 1) * BLOCK)
        for p in sb_pairs:
            gate = gb_ref[0, rows, p * LANES : (p + 1) * LANES].astype(jnp.float32)
            yb_ref[rows, p * LANES : (p + 1) * LANES] = (unstack_heads(acc_ref[sb, p]) * gate).astype(
                yb_ref.dtype
            )
    o_ref[0] += jnp.dot(yb_ref[...], wout_ref[SWA_WIDTH:, :], preferred_element_type=jnp.float32)


def _mixer(x, act, wout_bf16, tq):
    b, s, d = x.shape
    blk = lambda slot: pl.BlockSpec((1, tq, SLOT), lambda bi, i, slot=slot: (bi, i, slot))
    seq = lambda slot: pl.BlockSpec((1, s, SLOT), lambda bi, i, slot=slot: (bi, 0, slot))
    return pl.pallas_call(
        functools.partial(_mixer_kernel, tq=tq),
        out_shape=jax.ShapeDtypeStruct((b, s, d), jnp.float32),
        grid=(b, s // tq),
        in_specs=[
            pl.BlockSpec((1, tq, d), lambda bi, i: (bi, i, 0)),
            blk(SLOT_YA), blk(SLOT_QB), blk(SLOT_GB),
            seq(SLOT_KB), seq(SLOT_VB),
            pl.BlockSpec(wout_bf16.shape, lambda bi, i: (0, 0)),
        ],
        out_specs=pl.BlockSpec((1, tq, d), lambda bi, i: (bi, i, 0)),
        scratch_shapes=[
            pltpu.VMEM((tq, SB_WIDTH), jnp.bfloat16),
            pltpu.VMEM((tq // BLOCK, SB_PAIRS, 2 * BLOCK, LANES), jnp.float32),
            pltpu.VMEM((tq // BLOCK, SB_PAIRS, 2 * BLOCK, LANES), jnp.float32),
            pltpu.VMEM((tq // BLOCK, SUBLANES, LANES), jnp.float32),
            pltpu.SMEM((tq // BLOCK,), jnp.float32),
        ],
        compiler_params=pltpu.CompilerParams(
            dimension_semantics=("arbitrary", "arbitrary"), vmem_limit_bytes=VMEM_LIMIT_BYTES
        ),
        name="mixer",
    )(x, act, act, act, act, act, wout_bf16)


def kernel(x, positions, norm_gain, w_in, q_norm_gain, k_norm_gain, sinks, w_out):
    b, s, d = x.shape
    tile = 1024
    assert s % tile == 0 and s >= CHUNK
    depth = w_in.shape[0]
    half = HEAD_DIM // 2
    inv_freq = ROPE_THETA ** (-jnp.arange(half, dtype=jnp.float32) * 2.0 / HEAD_DIM)
    invf = jnp.tile(inv_freq, LANES // half).reshape(1, LANES)
    n_groups = LANES // half
    packed_pos = jnp.repeat(
        positions.reshape(-1, n_groups, tile // n_groups).transpose(0, 2, 1), half, axis=2
    ).reshape(-1, LANES)
    for l in range(depth):
        w_bf16 = w_in[l].astype(jnp.bfloat16)
        gain = norm_gain[l].reshape(1, d)
        act = _inproj(
            sinks[l].reshape(1, SWA_Q_HEADS),
            x.reshape(b * s, d),
            packed_pos,
            gain,
            w_bf16,
            jnp.tile(q_norm_gain[l], LANES // HEAD_DIM).reshape(1, LANES),
            jnp.tile(k_norm_gain[l], LANES // HEAD_DIM).reshape(1, LANES),
            invf,
            tm=tile,
            seq_len=s,
        )
        x = _mixer(x, act.reshape(b, s, ACT_WIDTH), w_out[l].astype(jnp.bfloat16), tq=tile)
    return x
```

```python
import functools
import math

import jax
import jax.numpy as jnp
from jax import lax
from jax.experimental import pallas as pl
from jax.experimental.pallas import tpu as pltpu

HEAD_DIM = 64
SWA_Q_HEADS = 8
SWA_KV_HEADS = 2
SB_HEADS = 8
BLOCK = 128
ROPE_THETA = 10000.0
EPS = 1e-6
LANES = 128
SUBLANES = 8
CHUNK = 256
LOG2E = math.log2(math.e)

SWA_WIDTH = SWA_Q_HEADS * HEAD_DIM
SWA_KV_WIDTH = SWA_KV_HEADS * HEAD_DIM
SB_WIDTH = SB_HEADS * HEAD_DIM
MIX_WIDTH = SWA_WIDTH + SB_WIDTH
SWA_PAIRS = SWA_Q_HEADS // 2
SB_PAIRS = SB_HEADS // 2

SRC_QA = 0
SRC_KA = SRC_QA + SWA_WIDTH
SRC_VA = SRC_KA + SWA_KV_WIDTH
SRC_GA = SRC_VA + SWA_KV_WIDTH
SRC_QB = SRC_GA + SWA_WIDTH
SRC_KB = SRC_QB + SB_WIDTH
SRC_VB = SRC_KB + SB_WIDTH
SRC_GB = SRC_VB + SB_WIDTH

SLOT = 512
SLOT_YA, SLOT_QB, SLOT_KB, SLOT_VB, SLOT_GB = range(5)
ACT_WIDTH = 5 * SLOT

VMEM_LIMIT_BYTES = 56 * 1024 * 1024

SB_DONE_LOG2 = 128.0
MASKED_SCORE = -1e30

_NT = (((1,), (1,)), ((), ()))


def _lane_iota(shape):
    return lax.broadcasted_iota(jnp.int32, shape, len(shape) - 1)


def _rmsnorm_bf16(x, gain):
    ms = jnp.mean(x * x, axis=-1, keepdims=True)
    return (x * lax.rsqrt(ms + EPS) * gain).astype(jnp.bfloat16)


def _silu(g):
    return g * (1.0 / (1.0 + jnp.exp(-g)))


def _softplus_log2(z2):
    return jnp.maximum(z2, 0.0) + jnp.log2(1.0 + jnp.exp2(-jnp.abs(z2)))


def _pair_masks():
    row2 = lax.broadcasted_iota(jnp.int32, (2 * BLOCK, LANES), 0)
    lane2 = lax.broadcasted_iota(jnp.int32, (2 * BLOCK, LANES), 1)
    own_head = (row2 < BLOCK) == (lane2 < HEAD_DIM)
    low = _lane_iota((1, LANES)) < HEAD_DIM

    def stack_heads(q_pair):
        q2 = jnp.concatenate([q_pair, q_pair], axis=0)
        return jnp.where(own_head, q2, jnp.zeros_like(q2))

    def unstack_heads(o2):
        return jnp.where(low, o2[:BLOCK], o2[BLOCK:])

    return row2, lane2, stack_heads, unstack_heads


def _inproj_kernel(
    sinks_ref, x_ref, pos_ref, gain_ref, w_ref, qgain_ref, kgain_ref, invf_ref,
    o_ref, h_ref, qa_ref, ga_ref, kv_ref, *, tiles_per_seq,
):
    tm = x_ref.shape[0]
    first_tile = (pl.program_id(0) % tiles_per_seq) == 0

    @pl.when(first_tile)
    def _():
        kv_ref[0:BLOCK, :] = jnp.zeros((BLOCK, kv_ref.shape[1]), kv_ref.dtype)

    h_ref[...] = _rmsnorm_bf16(x_ref[...], gain_ref[...])

    lane = _lane_iota((1, LANES))
    first_half = (lane % HEAD_DIM) < (HEAD_DIM // 2)
    low = lane < HEAD_DIM
    n_groups = LANES // (HEAD_DIM // 2)
    group = lane // (HEAD_DIM // 2)
    packed_ang = pos_ref[...].astype(jnp.float32) * invf_ref[...]

    def spread(packed):
        rolled = [packed] + [pltpu.roll(packed, s * (HEAD_DIM // 2), 1) for s in range(1, n_groups)]
        quarters = []
        for g in range(n_groups):
            t = rolled[(n_groups - 1 - g) % n_groups]
            for j in reversed(range(n_groups - 1)):
                t = jnp.where(group == j, rolled[(j - g) % n_groups], t)
            quarters.append(t)
        return jnp.concatenate(quarters, axis=0)

    cos = spread(jnp.cos(packed_ang))
    sin_signed = spread(jnp.sin(packed_ang)) * jnp.where(first_half, -1.0, 1.0)

    r = lax.broadcasted_iota(jnp.int32, (LANES, LANES), 0) // HEAD_DIM
    c = lax.broadcasted_iota(jnp.int32, (LANES, LANES), 1) // HEAD_DIM
    head_ones = jnp.where(r == c, 1.0, 0.0).astype(jnp.bfloat16)

    def norm_rope(a, head_gain, scale):
        ss = jnp.dot((a * a).astype(jnp.bfloat16), head_ones, preferred_element_type=jnp.float32)
        y = a * lax.rsqrt(ss * (1.0 / HEAD_DIM) + EPS) * head_gain
        partner = jnp.where(
            first_half, pltpu.roll(y, LANES - HEAD_DIM // 2, 1), pltpu.roll(y, HEAD_DIM // 2, 1)
        )
        y = y * cos + partner * sin_signed
        return y * scale if scale != 1.0 else y

    def project(src_col):
        return jnp.dot(h_ref[...], w_ref[:, src_col : src_col + CHUNK], preferred_element_type=jnp.float32)

    def store(slot, col, val):
        o_ref[:, slot * SLOT + col : slot * SLOT + col + val.shape[1]] = val.astype(o_ref.dtype)

    q_scale = LOG2E / math.sqrt(HEAD_DIM)

    def swa_queries(acc, col):
        for half in range(CHUNK // LANES):
            a = acc[:, half * LANES : (half + 1) * LANES]
            lanes = slice(col + half * LANES, col + (half + 1) * LANES)
            qa_ref[:, lanes] = norm_rope(a, qgain_ref[...], q_scale).astype(qa_ref.dtype)

    def swa_keys_values(acc):
        k = norm_rope(acc[:, :LANES], kgain_ref[...], 1.0)
        v = acc[:, LANES:]
        for base, t in ((0, k), (2 * LANES, v)):
            swapped = pltpu.roll(t, HEAD_DIM, 1)
            kv_ref[BLOCK:, base : base + LANES] = jnp.where(low, t, swapped).astype(kv_ref.dtype)
            kv_ref[BLOCK:, base + LANES : base + 2 * LANES] = jnp.where(low, swapped, t).astype(kv_ref.dtype)

    row2, lane2, stack_heads, unstack_heads = _pair_masks()
    in_cur_block = lane2 <= (row2 & (BLOCK - 1))
    kv_group = [(2 * p) // (SWA_Q_HEADS // SWA_KV_HEADS) for p in range(SWA_PAIRS)]

    def swa_scores(j):
        scores = []
        for p in range(SWA_PAIRS):
            q2 = stack_heads(qa_ref[j * BLOCK : (j + 1) * BLOCK, p * LANES : (p + 1) * LANES])
            k_win = kv_ref[j * BLOCK : (j + 2) * BLOCK, kv_group[p] * LANES : (kv_group[p] + 1) * LANES]
            scores.append(lax.dot_general(q2, k_win, _NT, preferred_element_type=jnp.float32))
        return scores

    def swa_softmax(j, scores):
        probs = []
        for p in range(SWA_PAIRS):
            s = jnp.where(in_cur_block, scores[p][:, BLOCK:], scores[p][:, :BLOCK])
            if j == 0:
                has_prev_cells = jnp.logical_not((jnp.zeros_like(lane2) + first_tile.astype(jnp.int32)) > 0)
                s = jnp.where(jnp.logical_or(in_cur_block, has_prev_cells), s, -jnp.inf)
            sink = jnp.where(
                row2[:, :1] < BLOCK, LOG2E * sinks_ref[0, 2 * p], LOG2E * sinks_ref[0, 2 * p + 1]
            )
            m = jnp.max(s, axis=-1, keepdims=True)
            e = jnp.exp2(s - m)
            denom = jnp.sum(e, axis=-1, keepdims=True) + jnp.exp2(sink - m)
            e_both = jnp.concatenate(
                [jnp.where(in_cur_block, 0.0, e), jnp.where(in_cur_block, e, 0.0)], axis=1
            ).astype(jnp.bfloat16)
            probs.append((e_both, denom))
        return probs

    def swa_values(j, probs):
        rows = slice(j * BLOCK, (j + 1) * BLOCK)
        for p in range(SWA_PAIRS):
            e_both, denom = probs[p]
            v_cols = slice((SWA_KV_HEADS + kv_group[p]) * LANES, (SWA_KV_HEADS + kv_group[p] + 1) * LANES)
            o2 = jnp.dot(e_both, kv_ref[j * BLOCK : (j + 2) * BLOCK, v_cols], preferred_element_type=jnp.float32)
            gate = ga_ref[rows, p * LANES : (p + 1) * LANES].astype(jnp.float32)
            o_ref[rows, SLOT_YA * SLOT + p * LANES : SLOT_YA * SLOT + (p + 1) * LANES] = (
                unstack_heads(o2 / denom) * gate
            ).astype(o_ref.dtype)

    first, second = 0, CHUNK
    qa_first = project(SRC_QA + first)
    kva = project(SRC_KA)
    qa_second = project(SRC_QA + second)
    ga_ref[:, first : first + CHUNK] = _silu(project(SRC_GA + first)).astype(ga_ref.dtype)
    swa_queries(qa_first, first)
    ga_ref[:, second : second + CHUNK] = _silu(project(SRC_GA + second)).astype(ga_ref.dtype)
    swa_keys_values(kva)
    half_rows = tm // 2
    scale_q = lambda v: v * q_scale
    plain_pieces = [
        (slot, src, col, r0, post)
        for slot, src, post in (
            (SLOT_QB, SRC_QB, scale_q), (SLOT_KB, SRC_KB, None), (SLOT_VB, SRC_VB, None), (SLOT_GB, SRC_GB, _silu)
        )
        for col in range(0, SLOT, CHUNK)
        for r0 in (0, half_rows)
    ]

    def plain_piece(slot, src, col, r0, post):
        val = jnp.dot(
            h_ref[r0 : r0 + half_rows], w_ref[:, src + col : src + col + CHUNK], preferred_element_type=jnp.float32
        )
        val = post(val) if post else val
        o_ref[r0 : r0 + half_rows, slot * SLOT + col : slot * SLOT + col + CHUNK] = val.astype(o_ref.dtype)

    swa_queries(qa_second, second)
    n_blk = tm // BLOCK
    n_slots = 2 * n_blk
    by_slot = [[] for _ in range(n_slots)]
    for k, piece in enumerate(plain_pieces):
        by_slot[k * n_slots // len(plain_pieces)].append(piece)

    scores = swa_scores(0)
    for j in range(n_blk):
        for piece in by_slot[2 * j]:
            plain_piece(*piece)
        probs = swa_softmax(j, scores)
        if j + 1 < n_blk:
            scores = swa_scores(j + 1)
        for piece in by_slot[2 * j + 1]:
            plain_piece(*piece)
        swa_values(j, probs)

    kv_ref[0:BLOCK, :] = kv_ref[tm : tm + BLOCK, :]


def _inproj(sinks, x2, pos2, gain, w_bf16, qgain, kgain, invf, tm, seq_len):
    n, d = x2.shape
    full = lambda i: (0, 0)
    return pl.pallas_call(
        functools.partial(_inproj_kernel, tiles_per_seq=seq_len // tm),
        out_shape=jax.ShapeDtypeStruct((n, ACT_WIDTH), jnp.bfloat16),
        grid=(n // tm,),
        in_specs=[
            pl.BlockSpec(memory_space=pltpu.SMEM),
            pl.BlockSpec((tm, d), lambda i: (i, 0)),
            pl.BlockSpec((tm // (LANES // (HEAD_DIM // 2)), LANES), lambda i: (i, 0)),
            pl.BlockSpec((1, d), full),
            pl.BlockSpec(w_bf16.shape, full),
            pl.BlockSpec((1, LANES), full),
            pl.BlockSpec((1, LANES), full),
            pl.BlockSpec((1, LANES), full),
        ],
        out_specs=pl.BlockSpec((tm, ACT_WIDTH), lambda i: (i, 0)),
        scratch_shapes=[
            pltpu.VMEM((tm, d), jnp.bfloat16),
            pltpu.VMEM((tm, SWA_WIDTH), jnp.bfloat16),
            pltpu.VMEM((tm, SWA_WIDTH), jnp.bfloat16),
            pltpu.VMEM((BLOCK + tm, 2 * SWA_KV_HEADS * LANES), jnp.bfloat16),
        ],
        compiler_params=pltpu.CompilerParams(
            dimension_semantics=("arbitrary",), vmem_limit_bytes=VMEM_LIMIT_BYTES
        ),
        name="inproj",
    )(sinks, x2, pos2, gain, w_bf16, qgain, kgain, invf)


def _mixer_kernel(
    x_ref, ya_ref, qb_ref, gb_ref, kb_ref, vb_ref, wout_ref,
    o_ref, yb_ref, acc_ref, fail_ref, lowest_ref, lowest_smem, *, tq, steps_per_seq,
):
    t = pl.program_id(0)
    last_step = pl.num_programs(0) - 1
    step = t % steps_per_seq
    n_sub = tq // BLOCK
    _, _, stack_heads, unstack_heads = _pair_masks()
    qrow_w = lax.broadcasted_iota(jnp.int32, (2 * BLOCK, CHUNK), 0) & (BLOCK - 1)
    key_w = lax.broadcasted_iota(jnp.int32, (2 * BLOCK, CHUNK), 1)
    kr = lax.broadcasted_iota(jnp.int32, (CHUNK, CHUNK), 0)
    kc = lax.broadcasted_iota(jnp.int32, (CHUNK, CHUNK), 1)
    suffix_ones = jnp.where(kr >= kc, 1.0, 0.0).astype(jnp.bfloat16)
    sb_pairs = range(SB_PAIRS)

    def sb_scores(p, q2, ks, valid):
        k_t = kb_ref[0, pl.ds(ks, CHUNK), p * LANES : (p + 1) * LANES]
        z = lax.dot_general(q2, k_t, _NT, preferred_element_type=jnp.float32)
        return jnp.where(valid, z, MASKED_SCORE)

    def sb_suffix(z):
        return jnp.dot(
            _softplus_log2(z).astype(jnp.bfloat16), suffix_ones, preferred_element_type=jnp.float32
        )

    def sb_accumulate(sb, p, z, upto, ks, first):
        arg = z - upto
        if not first:
            fail = fail_ref[sb, p]
            arg = arg - jnp.concatenate([fail] * (CHUNK // LANES), axis=1)
        w = jnp.exp2(arg).astype(jnp.bfloat16)
        v_t = vb_ref[0, pl.ds(ks, CHUNK), p * LANES : (p + 1) * LANES]
        pv = jnp.dot(w, v_t, preferred_element_type=jnp.float32)
        total = jnp.broadcast_to(upto[:, :1], (2 * BLOCK, LANES))
        if first:
            acc_ref[sb, p] = pv
            fail_ref[sb, p] = total
            return total
        acc_ref[sb, p] += pv
        fail_ref[sb, p] = fail + total
        return fail + total

    def block_index(sb):
        return step * n_sub + sb

    def sb_queries(r0):
        return [stack_heads(qb_ref[0, pl.ds(r0, BLOCK), p * LANES : (p + 1) * LANES]) for p in sb_pairs]

    def stage_scores(sb):
        blk = block_index(sb)
        kp = pl.multiple_of(jnp.maximum(blk - 1, 0) * BLOCK, BLOCK)
        window_valid = key_w < (qrow_w + jnp.where(blk > 0, BLOCK, 0))
        zs = [sb_scores(p, q2, kp, window_valid) for p, q2 in enumerate(sb_queries(sb * BLOCK))]
        return dict(blk=blk, kp=kp, zs=zs)

    def stage_suffix(st):
        st["uptos"] = [sb_suffix(z) for z in st["zs"]]

    def stage_values(sb, st):
        lowest = None
        for p in sb_pairs:
            f = sb_accumulate(sb, p, st["zs"][p], st["uptos"][p], st["kp"], first=True)
            lowest = f if lowest is None else jnp.minimum(lowest, f)
        lowest = jnp.where(st["blk"] >= 2, lowest, jnp.inf)
        lowest_ref[sb] = jnp.min(lowest.reshape(-1, SUBLANES, LANES), axis=0)

    def finish_chunk(piece):
        c, r0 = piece
        rows = slice(r0, r0 + tq // 2)
        o_ref[0, rows, c : c + CHUNK] = (
            x_ref[0, rows, c : c + CHUNK]
            + jnp.dot(ya_ref[0, rows], wout_ref[:SWA_WIDTH, c : c + CHUNK], preferred_element_type=jnp.float32)
            + jnp.dot(yb_ref[rows], wout_ref[SWA_WIDTH:, c : c + CHUNK], preferred_element_type=jnp.float32)
        )

    all_chunks = [(c, r0) for c in range(0, o_ref.shape[2], CHUNK) for r0 in (0, tq // 2)]

    def attend(out_chunks):
        def next_out_chunk():
            if out_chunks:
                finish_chunk(out_chunks.pop(0))

        stage = stage_scores(0)
        for sb in range(n_sub):
            next_out_chunk()
            stage_suffix(stage)
            following = stage_scores(sb + 1) if sb + 1 < n_sub else None
            stage_values(sb, stage)
            stage = following
        while out_chunks:
            next_out_chunk()

        pl.when(jnp.min(lowest_ref[...]) < SB_DONE_LOG2)(visit_earlier_tiles)

        for sb in range(n_sub):
            rows = slice(sb * BLOCK, (sb + 1) * BLOCK)
            for p in sb_pairs:
                gate = gb_ref[0, rows, p * LANES : (p + 1) * LANES].astype(jnp.float32)
                yb_ref[rows, p * LANES : (p + 1) * LANES] = (
                    unstack_heads(acc_ref[sb, p]) * gate
                ).astype(yb_ref.dtype)

    def visit_earlier_tiles():
        for sb in range(n_sub):
            lowest_smem[sb] = jnp.min(lowest_ref[sb])

        def earlier_tiles(sb, carry):
            blk = block_index(sb)
            q2s = sb_queries(pl.multiple_of(sb * BLOCK, BLOCK))
            n_tiles = blk // 2

            def cond(c):
                n, lowest = c
                return jnp.logical_and(n < n_tiles, lowest < SB_DONE_LOG2)

            def body(c):
                n, _ = c
                start = (blk - 1) * BLOCK - CHUNK * (n + 1)
                ks = pl.multiple_of(jnp.maximum(start, 0), BLOCK)
                tile_valid = key_w < (CHUNK + jnp.minimum(start, 0))
                zs = [sb_scores(p, q2s[p], ks, tile_valid) for p in sb_pairs]
                uptos = [sb_suffix(z) for z in zs]
                lowest = None
                for p in sb_pairs:
                    f = sb_accumulate(sb, p, zs[p], uptos[p], ks, first=False)
                    lowest = f if lowest is None else jnp.minimum(lowest, f)
                return n + 1, jnp.min(lowest)

            lax.while_loop(cond, body, (jnp.int32(0), lowest_smem[sb]))
            return carry

        lax.fori_loop(0, n_sub, earlier_tiles, 0)

    @pl.when(t == 0)
    def _():
        yb_ref[...] = jnp.zeros(yb_ref.shape, yb_ref.dtype)

    @pl.when(t < last_step)
    def _():
        attend(list(all_chunks))

    @pl.when(t == last_step)
    def _():
        for c in all_chunks:
            finish_chunk(c)


def _mixer(x, act, wout_bf16, tq):
    b, s, d = x.shape
    steps_per_seq = s // tq
    n_tiles = b * steps_per_seq

    def attended(t, last):
        tile = jnp.minimum(t, n_tiles - 1)
        return tile // steps_per_seq, tile % steps_per_seq, last

    def finished(t, last):
        tile = jnp.maximum(t - 1, 0)
        return tile // steps_per_seq, tile % steps_per_seq, last

    now = lambda slot: pl.BlockSpec((1, tq, SLOT), lambda t, slot=slot: attended(t, slot))
    seq = lambda slot: pl.BlockSpec((1, s, SLOT), lambda t, slot=slot: (attended(t, slot)[0], 0, slot))
    return pl.pallas_call(
        functools.partial(_mixer_kernel, tq=tq, steps_per_seq=steps_per_seq),
        out_shape=jax.ShapeDtypeStruct((b, s, d), jnp.float32),
        grid=(n_tiles + 1,),
        in_specs=[
            pl.BlockSpec((1, tq, d), lambda t: finished(t, 0)),
            pl.BlockSpec((1, tq, SLOT), lambda t: finished(t, SLOT_YA)),
            now(SLOT_QB), now(SLOT_GB),
            seq(SLOT_KB), seq(SLOT_VB),
            pl.BlockSpec(wout_bf16.shape, lambda t: (0, 0)),
        ],
        out_specs=pl.BlockSpec((1, tq, d), lambda t: finished(t, 0)),
        scratch_shapes=[
            pltpu.VMEM((tq, SB_WIDTH), jnp.bfloat16),
            pltpu.VMEM((tq // BLOCK, SB_PAIRS, 2 * BLOCK, LANES), jnp.float32),
            pltpu.VMEM((tq // BLOCK, SB_PAIRS, 2 * BLOCK, LANES), jnp.float32),
            pltpu.VMEM((tq // BLOCK, SUBLANES, LANES), jnp.float32),
            pltpu.SMEM((tq // BLOCK,), jnp.float32),
        ],
        compiler_params=pltpu.CompilerParams(
            dimension_semantics=("arbitrary",), vmem_limit_bytes=VMEM_LIMIT_BYTES
        ),
        name="mixer",
    )(x, act, act, act, act, act, wout_bf16)


def kernel(x, positions, norm_gain, w_in, q_norm_gain, k_norm_gain, sinks, w_out):
    b, s, d = x.shape
    tile = 1024
    assert s % tile == 0 and s >= CHUNK
    depth = w_in.shape[0]
    half = HEAD_DIM // 2
    inv_freq = ROPE_THETA ** (-jnp.arange(half, dtype=jnp.float32) * 2.0 / HEAD_DIM)
    invf = jnp.tile(inv_freq, LANES // half).reshape(1, LANES)
    n_groups = LANES // half
    packed_pos = jnp.repeat(
        positions.reshape(-1, n_groups, tile // n_groups).transpose(0, 2, 1), half, axis=2
    ).reshape(-1, LANES)
    for l in range(depth):
        w_bf16 = w_in[l].astype(jnp.bfloat16)
        gain = norm_gain[l].reshape(1, d)
        act = _inproj(
            sinks[l].reshape(1, SWA_Q_HEADS),
            x.reshape(b * s, d),
            packed_pos,
            gain,
            w_bf16,
            jnp.tile(q_norm_gain[l], LANES // HEAD_DIM).reshape(1, LANES),
            jnp.tile(k_norm_gain[l], LANES // HEAD_DIM).reshape(1, LANES),
            invf,
            tm=tile,
            seq_len=s,
        )
        x = _mixer(x, act.reshape(b, s, ACT_WIDTH), w_out[l].astype(jnp.bfloat16), tq=tile)
    return x
```

```python
import functools
import math

import jax
import jax.numpy as jnp
from jax import lax
from jax.experimental import pallas as pl
from jax.experimental.pallas import tpu as pltpu

HEAD_DIM = 64
SWA_Q_HEADS = 8
SWA_KV_HEADS = 2
SB_HEADS = 8
BLOCK = 128
ROPE_THETA = 10000.0
EPS = 1e-6
LANES = 128
SUBLANES = 8
CHUNK = 256
LOG2E = math.log2(math.e)

SWA_WIDTH = SWA_Q_HEADS * HEAD_DIM
SWA_KV_WIDTH = SWA_KV_HEADS * HEAD_DIM
SB_WIDTH = SB_HEADS * HEAD_DIM
MIX_WIDTH = SWA_WIDTH + SB_WIDTH
SWA_PAIRS = SWA_Q_HEADS // 2
SB_PAIRS = SB_HEADS // 2

SRC_QA = 0
SRC_KA = SRC_QA + SWA_WIDTH
SRC_VA = SRC_KA + SWA_KV_WIDTH
SRC_GA = SRC_VA + SWA_KV_WIDTH
SRC_QB = SRC_GA + SWA_WIDTH
SRC_KB = SRC_QB + SB_WIDTH
SRC_VB = SRC_KB + SB_WIDTH
SRC_GB = SRC_VB + SB_WIDTH

SLOT = 512
SLOT_YA, SLOT_QB, SLOT_KB, SLOT_VB, SLOT_GB = range(5)
ACT_WIDTH = 5 * SLOT

VMEM_LIMIT_BYTES = 56 * 1024 * 1024

SB_DONE_LOG2 = 128.0
MASKED_SCORE = -1e30

_NT = (((1,), (1,)), ((), ()))


def _lane_iota(shape):
    return lax.broadcasted_iota(jnp.int32, shape, len(shape) - 1)


def _rmsnorm_bf16(x, gain):
    ms = jnp.mean(x * x, axis=-1, keepdims=True)
    return (x * lax.rsqrt(ms + EPS) * gain).astype(jnp.bfloat16)


def _silu(g):
    return g * (1.0 / (1.0 + jnp.exp(-g)))


def _softplus_log2(z2):
    return jnp.maximum(z2, 0.0) + jnp.log2(1.0 + jnp.exp2(-jnp.abs(z2)))


def _pair_masks():
    row2 = lax.broadcasted_iota(jnp.int32, (2 * BLOCK, LANES), 0)
    lane2 = lax.broadcasted_iota(jnp.int32, (2 * BLOCK, LANES), 1)
    own_head = (row2 < BLOCK) == (lane2 < HEAD_DIM)
    low = _lane_iota((1, LANES)) < HEAD_DIM

    def stack_heads(q_pair):
        q2 = jnp.concatenate([q_pair, q_pair], axis=0)
        return jnp.where(own_head, q2, jnp.zeros_like(q2))

    def unstack_heads(o2):
        return jnp.where(low, o2[:BLOCK], o2[BLOCK:])

    return row2, lane2, stack_heads, unstack_heads


def _inproj_kernel(
    sinks_ref, x_ref, pos_ref, gain_ref, w_ref, qgain_ref, kgain_ref, invf_ref,
    o_ref, h_ref, qa_ref, ga_ref, kv_ref, *, tiles_per_seq,
):
    tm = x_ref.shape[0]
    first_tile = (pl.program_id(0) % tiles_per_seq) == 0

    @pl.when(first_tile)
    def _():
        kv_ref[0:BLOCK, :] = jnp.zeros((BLOCK, kv_ref.shape[1]), kv_ref.dtype)

    h_ref[...] = _rmsnorm_bf16(x_ref[...], gain_ref[...])

    lane = _lane_iota((1, LANES))
    first_half = (lane % HEAD_DIM) < (HEAD_DIM // 2)
    low = lane < HEAD_DIM
    n_groups = LANES // (HEAD_DIM // 2)
    group = lane // (HEAD_DIM // 2)
    packed_ang = pos_ref[...].astype(jnp.float32) * invf_ref[...]

    def spread(packed):
        rolled = [packed] + [pltpu.roll(packed, s * (HEAD_DIM // 2), 1) for s in range(1, n_groups)]
        quarters = []
        for g in range(n_groups):
            t = rolled[(n_groups - 1 - g) % n_groups]
            for j in reversed(range(n_groups - 1)):
                t = jnp.where(group == j, rolled[(j - g) % n_groups], t)
            quarters.append(t)
        return jnp.concatenate(quarters, axis=0)

    cos = spread(jnp.cos(packed_ang))
    sin_signed = spread(jnp.sin(packed_ang)) * jnp.where(first_half, -1.0, 1.0)

    r = lax.broadcasted_iota(jnp.int32, (LANES, LANES), 0) // HEAD_DIM
    c = lax.broadcasted_iota(jnp.int32, (LANES, LANES), 1) // HEAD_DIM
    head_ones = jnp.where(r == c, 1.0, 0.0).astype(jnp.bfloat16)

    def norm_rope(a, head_gain, scale):
        ss = jnp.dot((a * a).astype(jnp.bfloat16), head_ones, preferred_element_type=jnp.float32)
        y = a * lax.rsqrt(ss * (1.0 / HEAD_DIM) + EPS) * head_gain
        partner = jnp.where(
            first_half, pltpu.roll(y, LANES - HEAD_DIM // 2, 1), pltpu.roll(y, HEAD_DIM // 2, 1)
        )
        y = y * cos + partner * sin_signed
        return y * scale if scale != 1.0 else y

    def project(src_col):
        return jnp.dot(h_ref[...], w_ref[:, src_col : src_col + CHUNK], preferred_element_type=jnp.float32)

    def store(slot, col, val):
        o_ref[:, slot * SLOT + col : slot * SLOT + col + val.shape[1]] = val.astype(o_ref.dtype)

    q_scale = LOG2E / math.sqrt(HEAD_DIM)

    def swa_queries(acc, col):
        for half in range(CHUNK // LANES):
            a = acc[:, half * LANES : (half + 1) * LANES]
            lanes = slice(col + half * LANES, col + (half + 1) * LANES)
            qa_ref[:, lanes] = norm_rope(a, qgain_ref[...], q_scale).astype(qa_ref.dtype)

    def swa_keys_values(acc):
        k = norm_rope(acc[:, :LANES], kgain_ref[...], 1.0)
        v = acc[:, LANES:]
        for base, t in ((0, k), (2 * LANES, v)):
            swapped = pltpu.roll(t, HEAD_DIM, 1)
            kv_ref[BLOCK:, base : base + LANES] = jnp.where(low, t, swapped).astype(kv_ref.dtype)
            kv_ref[BLOCK:, base + LANES : base + 2 * LANES] = jnp.where(low, swapped, t).astype(kv_ref.dtype)

    row2, lane2, stack_heads, unstack_heads = _pair_masks()
    in_cur_block = lane2 <= (row2 & (BLOCK - 1))
    kv_group = [(2 * p) // (SWA_Q_HEADS // SWA_KV_HEADS) for p in range(SWA_PAIRS)]

    def swa_scores(j):
        scores = []
        for p in range(SWA_PAIRS):
            q2 = stack_heads(qa_ref[j * BLOCK : (j + 1) * BLOCK, p * LANES : (p + 1) * LANES])
            k_win = kv_ref[j * BLOCK : (j + 2) * BLOCK, kv_group[p] * LANES : (kv_group[p] + 1) * LANES]
            scores.append(lax.dot_general(q2, k_win, _NT, preferred_element_type=jnp.float32))
        return scores

    def swa_softmax(j, scores):
        probs = []
        for p in range(SWA_PAIRS):
            s = jnp.where(in_cur_block, scores[p][:, BLOCK:], scores[p][:, :BLOCK])
            if j == 0:
                has_prev_cells = jnp.logical_not((jnp.zeros_like(lane2) + first_tile.astype(jnp.int32)) > 0)
                s = jnp.where(jnp.logical_or(in_cur_block, has_prev_cells), s, -jnp.inf)
            sink = jnp.where(
                row2[:, :1] < BLOCK, LOG2E * sinks_ref[0, 2 * p], LOG2E * sinks_ref[0, 2 * p + 1]
            )
            m = jnp.max(s, axis=-1, keepdims=True)
            e = jnp.exp2(s - m)
            denom = jnp.sum(e, axis=-1, keepdims=True) + jnp.exp2(sink - m)
            e_both = jnp.concatenate(
                [jnp.where(in_cur_block, 0.0, e), jnp.where(in_cur_block, e, 0.0)], axis=1
            ).astype(jnp.bfloat16)
            probs.append((e_both, denom))
        return probs

    def swa_values(j, probs):
        rows = slice(j * BLOCK, (j + 1) * BLOCK)
        for p in range(SWA_PAIRS):
            e_both, denom = probs[p]
            v_cols = slice((SWA_KV_HEADS + kv_group[p]) * LANES, (SWA_KV_HEADS + kv_group[p] + 1) * LANES)
            o2 = jnp.dot(e_both, kv_ref[j * BLOCK : (j + 2) * BLOCK, v_cols], preferred_element_type=jnp.float32)
            gate = ga_ref[rows, p * LANES : (p + 1) * LANES].astype(jnp.float32)
            o_ref[rows, SLOT_YA * SLOT + p * LANES : SLOT_YA * SLOT + (p + 1) * LANES] = (
                unstack_heads(o2 / denom) * gate
            ).astype(o_ref.dtype)

    first, second = 0, CHUNK
    qa_first = project(SRC_QA + first)
    kva = project(SRC_KA)
    qa_second = project(SRC_QA + second)
    ga_ref[:, first : first + CHUNK] = _silu(project(SRC_GA + first)).astype(ga_ref.dtype)
    swa_queries(qa_first, first)
    ga_ref[:, second : second + CHUNK] = _silu(project(SRC_GA + second)).astype(ga_ref.dtype)
    swa_keys_values(kva)
    half_rows = tm // 2
    scale_q = lambda v: v * q_scale
    plain_pieces = [
        (slot, src, col, r0, post)
        for slot, src, post in (
            (SLOT_QB, SRC_QB, scale_q), (SLOT_KB, SRC_KB, None), (SLOT_VB, SRC_VB, None), (SLOT_GB, SRC_GB, _silu)
        )
        for col in range(0, SLOT, CHUNK)
        for r0 in (0, half_rows)
    ]

    def plain_piece(slot, src, col, r0, post):
        val = jnp.dot(
            h_ref[r0 : r0 + half_rows], w_ref[:, src + col : src + col + CHUNK], preferred_element_type=jnp.float32
        )
        val = post(val) if post else val
        o_ref[r0 : r0 + half_rows, slot * SLOT + col : slot * SLOT + col + CHUNK] = val.astype(o_ref.dtype)

    swa_queries(qa_second, second)
    n_blk = tm // BLOCK
    n_slots = 2 * n_blk
    by_slot = [[] for _ in range(n_slots)]
    for k, piece in enumerate(plain_pieces):
        by_slot[k * n_slots // len(plain_pieces)].append(piece)

    scores = swa_scores(0)
    for j in range(n_blk):
        for piece in by_slot[2 * j]:
            plain_piece(*piece)
        probs = swa_softmax(j, scores)
        if j + 1 < n_blk:
            scores = swa_scores(j + 1)
        for piece in by_slot[2 * j + 1]:
            plain_piece(*piece)
        swa_values(j, probs)

    kv_ref[0:BLOCK, :] = kv_ref[tm : tm + BLOCK, :]


def _inproj(sinks, x2, pos2, gain, w_bf16, qgain, kgain, invf, tm, seq_len):
    n, d = x2.shape
    full = lambda i: (0, 0)
    return pl.pallas_call(
        functools.partial(_inproj_kernel, tiles_per_seq=seq_len // tm),
        out_shape=jax.ShapeDtypeStruct((n, ACT_WIDTH), jnp.bfloat16),
        grid=(n // tm,),
        in_specs=[
            pl.BlockSpec(memory_space=pltpu.SMEM),
            pl.BlockSpec((tm, d), lambda i: (i, 0)),
            pl.BlockSpec((tm // (LANES // (HEAD_DIM // 2)), LANES), lambda i: (i, 0)),
            pl.BlockSpec((1, d), full),
            pl.BlockSpec(w_bf16.shape, full),
            pl.BlockSpec((1, LANES), full),
            pl.BlockSpec((1, LANES), full),
            pl.BlockSpec((1, LANES), full),
        ],
        out_specs=pl.BlockSpec((tm, ACT_WIDTH), lambda i: (i, 0)),
        scratch_shapes=[
            pltpu.VMEM((tm, d), jnp.bfloat16),
            pltpu.VMEM((tm, SWA_WIDTH), jnp.bfloat16),
            pltpu.VMEM((tm, SWA_WIDTH), jnp.bfloat16),
            pltpu.VMEM((BLOCK + tm, 2 * SWA_KV_HEADS * LANES), jnp.bfloat16),
        ],
        compiler_params=pltpu.CompilerParams(
            dimension_semantics=("arbitrary",), vmem_limit_bytes=VMEM_LIMIT_BYTES
        ),
        name="inproj",
    )(sinks, x2, pos2, gain, w_bf16, qgain, kgain, invf)


def _mixer_kernel(
    x_ref, ya_ref, qb_ref, gb_ref, kb_ref, vb_ref, wout_ref,
    o_ref, yb_ref, acc_ref, fail_ref, lowest_ref, lowest_smem, *, tq, steps_per_seq,
):
    t = pl.program_id(0)
    last_step = pl.num_programs(0) - 1
    step = t % steps_per_seq
    n_sub = tq // BLOCK
    row2, lane2, stack_heads, unstack_heads = _pair_masks()
    qrow_w = lax.broadcasted_iota(jnp.int32, (2 * BLOCK, CHUNK), 0) & (BLOCK - 1)
    key_w = lax.broadcasted_iota(jnp.int32, (2 * BLOCK, CHUNK), 1)
    causal_bias = jnp.where(lane2 < (row2 & (BLOCK - 1)), 0.0, MASKED_SCORE)
    kr = lax.broadcasted_iota(jnp.int32, (CHUNK, CHUNK), 0)
    kc = lax.broadcasted_iota(jnp.int32, (CHUNK, CHUNK), 1)
    suffix_ones = jnp.where(kr >= kc, 1.0, 0.0).astype(jnp.bfloat16)
    sb_pairs = range(SB_PAIRS)

    def sb_scores(p, q2, ks, mask):
        k_t = kb_ref[0, pl.ds(ks, CHUNK), p * LANES : (p + 1) * LANES]
        z = lax.dot_general(q2, k_t, _NT, preferred_element_type=jnp.float32)
        return mask(z)

    def mask_where(valid):
        return lambda z: jnp.where(valid, z, MASKED_SCORE)

    def mask_second_block_causal(z):
        return jnp.concatenate([z[:, :BLOCK], z[:, BLOCK:] + causal_bias], axis=1)

    def sb_suffix(z):
        return jnp.dot(
            _softplus_log2(z).astype(jnp.bfloat16), suffix_ones, preferred_element_type=jnp.float32
        )

    def sb_accumulate(sb, p, z, upto, ks, first):
        arg = z - upto
        if not first:
            fail = fail_ref[sb, p]
            arg = arg - jnp.concatenate([fail] * (CHUNK // LANES), axis=1)
        w = jnp.exp2(arg).astype(jnp.bfloat16)
        v_t = vb_ref[0, pl.ds(ks, CHUNK), p * LANES : (p + 1) * LANES]
        pv = jnp.dot(w, v_t, preferred_element_type=jnp.float32)
        total = jnp.broadcast_to(upto[:, :1], (2 * BLOCK, LANES))
        if first:
            acc_ref[sb, p] = pv
            fail_ref[sb, p] = total
            return total
        acc_ref[sb, p] += pv
        fail_ref[sb, p] = fail + total
        return fail + total

    def block_index(sb):
        return step * n_sub + sb

    def sb_queries(r0):
        return [stack_heads(qb_ref[0, pl.ds(r0, BLOCK), p * LANES : (p + 1) * LANES]) for p in sb_pairs]

    def stage_scores(sb):
        blk = block_index(sb)
        kp = pl.multiple_of(jnp.maximum(blk - 1, 0) * BLOCK, BLOCK)
        if sb == 0:
            mask = mask_where(key_w < (qrow_w + jnp.where(blk > 0, BLOCK, 0)))
        else:
            mask = mask_second_block_causal
        zs = [sb_scores(p, q2, kp, mask) for p, q2 in enumerate(sb_queries(sb * BLOCK))]
        return dict(blk=blk, kp=kp, zs=zs)

    def stage_suffix(st):
        st["uptos"] = [sb_suffix(z) for z in st["zs"]]

    def stage_values(sb, st):
        lowest = None
        for p in sb_pairs:
            f = sb_accumulate(sb, p, st["zs"][p], st["uptos"][p], st["kp"], first=True)
            lowest = f if lowest is None else jnp.minimum(lowest, f)
        if sb < 2:
            lowest = jnp.where(st["blk"] >= 2, lowest, jnp.inf)
        lowest_ref[sb] = jnp.min(lowest.reshape(-1, SUBLANES, LANES), axis=0)

    def finish_chunk(piece):
        c, r0 = piece
        rows = slice(r0, r0 + tq // 2)
        o_ref[0, rows, c : c + CHUNK] = (
            x_ref[0, rows, c : c + CHUNK]
            + jnp.dot(ya_ref[0, rows], wout_ref[:SWA_WIDTH, c : c + CHUNK], preferred_element_type=jnp.float32)
            + jnp.dot(yb_ref[rows], wout_ref[SWA_WIDTH:, c : c + CHUNK], preferred_element_type=jnp.float32)
        )

    all_chunks = [(c, r0) for c in range(0, o_ref.shape[2], CHUNK) for r0 in (0, tq // 2)]

    def attend(out_chunks):
        def next_out_chunk():
            if out_chunks:
                finish_chunk(out_chunks.pop(0))

        stage = stage_scores(0)
        for sb in range(n_sub):
            next_out_chunk()
            stage_suffix(stage)
            following = stage_scores(sb + 1) if sb + 1 < n_sub else None
            stage_values(sb, stage)
            stage = following
        while out_chunks:
            next_out_chunk()

        pl.when(jnp.min(lowest_ref[...]) < SB_DONE_LOG2)(visit_earlier_tiles)

        for sb in range(n_sub):
            rows = slice(sb * BLOCK, (sb + 1) * BLOCK)
            for p in sb_pairs:
                gate = gb_ref[0, rows, p * LANES : (p + 1) * LANES].astype(jnp.float32)
                yb_ref[rows, p * LANES : (p + 1) * LANES] = (
                    unstack_heads(acc_ref[sb, p]) * gate
                ).astype(yb_ref.dtype)

    def visit_earlier_tiles():
        for sb in range(n_sub):
            lowest_smem[sb] = jnp.min(lowest_ref[sb])

        def earlier_tiles(sb, carry):
            blk = block_index(sb)
            q2s = sb_queries(pl.multiple_of(sb * BLOCK, BLOCK))
            n_tiles = blk // 2

            def cond(c):
                n, lowest = c
                return jnp.logical_and(n < n_tiles, lowest < SB_DONE_LOG2)

            def body(c):
                n, _ = c
                start = (blk - 1) * BLOCK - CHUNK * (n + 1)
                ks = pl.multiple_of(jnp.maximum(start, 0), BLOCK)
                tile_valid = key_w < (CHUNK + jnp.minimum(start, 0))
                zs = [sb_scores(p, q2s[p], ks, mask_where(tile_valid)) for p in sb_pairs]
                uptos = [sb_suffix(z) for z in zs]
                lowest = None
                for p in sb_pairs:
                    f = sb_accumulate(sb, p, zs[p], uptos[p], ks, first=False)
                    lowest = f if lowest is None else jnp.minimum(lowest, f)
                return n + 1, jnp.min(lowest)

            lax.while_loop(cond, body, (jnp.int32(0), lowest_smem[sb]))
            return carry

        lax.fori_loop(0, n_sub, earlier_tiles, 0)

    @pl.when(t == 0)
    def _():
        yb_ref[...] = jnp.zeros(yb_ref.shape, yb_ref.dtype)

    @pl.when(t < last_step)
    def _():
        attend(list(all_chunks))

    @pl.when(t == last_step)
    def _():
        for c in all_chunks:
            finish_chunk(c)


def _mixer(x, act, wout_bf16, tq):
    b, s, d = x.shape
    steps_per_seq = s // tq
    n_tiles = b * steps_per_seq

    def attended(t, last):
        tile = jnp.minimum(t, n_tiles - 1)
        return tile // steps_per_seq, tile % steps_per_seq, last

    def finished(t, last):
        tile = jnp.maximum(t - 1, 0)
        return tile // steps_per_seq, tile % steps_per_seq, last

    now = lambda slot: pl.BlockSpec((1, tq, SLOT), lambda t, slot=slot: attended(t, slot))
    seq = lambda slot: pl.BlockSpec((1, s, SLOT), lambda t, slot=slot: (attended(t, slot)[0], 0, slot))
    return pl.pallas_call(
        functools.partial(_mixer_kernel, tq=tq, steps_per_seq=steps_per_seq),
        out_shape=jax.ShapeDtypeStruct((b, s, d), jnp.float32),
        grid=(n_tiles + 1,),
        in_specs=[
            pl.BlockSpec((1, tq, d), lambda t: finished(t, 0)),
            pl.BlockSpec((1, tq, SLOT), lambda t: finished(t, SLOT_YA)),
            now(SLOT_QB), now(SLOT_GB),
            seq(SLOT_KB), seq(SLOT_VB),
            pl.BlockSpec(wout_bf16.shape, lambda t: (0, 0)),
        ],
        out_specs=pl.BlockSpec((1, tq, d), lambda t: finished(t, 0)),
        scratch_shapes=[
            pltpu.VMEM((tq, SB_WIDTH), jnp.bfloat16),
            pltpu.VMEM((tq // BLOCK, SB_PAIRS, 2 * BLOCK, LANES), jnp.float32),
            pltpu.VMEM((tq // BLOCK, SB_PAIRS, 2 * BLOCK, LANES), jnp.float32),
            pltpu.VMEM((tq // BLOCK, SUBLANES, LANES), jnp.float32),
            pltpu.SMEM((tq // BLOCK,), jnp.float32),
        ],
        compiler_params=pltpu.CompilerParams(
            dimension_semantics=("arbitrary",), vmem_limit_bytes=VMEM_LIMIT_BYTES
        ),
        name="mixer",
    )(x, act, act, act, act, act, wout_bf16)


def kernel(x, positions, norm_gain, w_in, q_norm_gain, k_norm_gain, sinks, w_out):
    b, s, d = x.shape
    tile = 1024
    assert s % tile == 0 and s >= CHUNK
    depth = w_in.shape[0]
    half = HEAD_DIM // 2
    inv_freq = ROPE_THETA ** (-jnp.arange(half, dtype=jnp.float32) * 2.0 / HEAD_DIM)
    invf = jnp.tile(inv_freq, LANES // half).reshape(1, LANES)
    n_groups = LANES // half
    packed_pos = jnp.repeat(
        positions.reshape(-1, n_groups, tile // n_groups).transpose(0, 2, 1), half, axis=2
    ).reshape(-1, LANES)
    for l in range(depth):
        w_bf16 = w_in[l].astype(jnp.bfloat16)
        gain = norm_gain[l].reshape(1, d)
        act = _inproj(
            sinks[l].reshape(1, SWA_Q_HEADS),
            x.reshape(b * s, d),
            packed_pos,
            gain,
            w_bf16,
            jnp.tile(q_norm_gain[l], LANES // HEAD_DIM).reshape(1, LANES),
            jnp.tile(k_norm_gain[l], LANES // HEAD_DIM).reshape(1, LANES),
            invf,
            tm=tile,
            seq_len=s,
        )
        x = _mixer(x, act.reshape(b, s, ACT_WIDTH), w_out[l].astype(jnp.bfloat16), tq=tile)
    return x
```

```python
import functools
import math

import jax
import jax.numpy as jnp
from jax import lax
from jax.experimental import pallas as pl
from jax.experimental.pallas import tpu as pltpu

HEAD_DIM = 64
SWA_Q_HEADS = 8
SWA_KV_HEADS = 2
SB_HEADS = 8
BLOCK = 128
ROPE_THETA = 10000.0
EPS = 1e-6
LANES = 128
SUBLANES = 8
CHUNK = 256
LOG2E = math.log2(math.e)

SWA_WIDTH = SWA_Q_HEADS * HEAD_DIM
SWA_KV_WIDTH = SWA_KV_HEADS * HEAD_DIM
SB_WIDTH = SB_HEADS * HEAD_DIM
MIX_WIDTH = SWA_WIDTH + SB_WIDTH
SWA_PAIRS = SWA_Q_HEADS // 2
SB_PAIRS = SB_HEADS // 2

SRC_QA = 0
SRC_KA = SRC_QA + SWA_WIDTH
SRC_VA = SRC_KA + SWA_KV_WIDTH
SRC_GA = SRC_VA + SWA_KV_WIDTH
SRC_QB = SRC_GA + SWA_WIDTH
SRC_KB = SRC_QB + SB_WIDTH
SRC_VB = SRC_KB + SB_WIDTH
SRC_GB = SRC_VB + SB_WIDTH

SLOT = 512
SLOT_YA, SLOT_QB, SLOT_KB, SLOT_VB, SLOT_GB = range(5)
ACT_WIDTH = 5 * SLOT

VMEM_LIMIT_BYTES = 56 * 1024 * 1024

SB_DONE_LOG2 = 128.0
MASKED_SCORE = -1e30

_NT = (((1,), (1,)), ((), ()))


def _lane_iota(shape):
    return lax.broadcasted_iota(jnp.int32, shape, len(shape) - 1)


def _rmsnorm_bf16(x, gain):
    ms = jnp.mean(x * x, axis=-1, keepdims=True)
    return (x * lax.rsqrt(ms + EPS) * gain).astype(jnp.bfloat16)


def _silu(g):
    return g * (1.0 / (1.0 + jnp.exp(-g)))


def _softplus_log2(z2):
    return jnp.maximum(z2, 0.0) + jnp.log2(1.0 + jnp.exp2(-jnp.abs(z2)))


def _pair_masks():
    row2 = lax.broadcasted_iota(jnp.int32, (2 * BLOCK, LANES), 0)
    lane2 = lax.broadcasted_iota(jnp.int32, (2 * BLOCK, LANES), 1)
    own_head = (row2 < BLOCK) == (lane2 < HEAD_DIM)
    low = _lane_iota((1, LANES)) < HEAD_DIM

    def stack_heads(q_pair):
        q2 = jnp.concatenate([q_pair, q_pair], axis=0)
        return jnp.where(own_head, q2, jnp.zeros_like(q2))

    def unstack_heads(o2):
        return jnp.where(low, o2[:BLOCK], o2[BLOCK:])

    return row2, lane2, stack_heads, unstack_heads


def _inproj_kernel(
    sinks_ref, x_ref, pos_ref, gain_ref, w_ref, qgain_ref, kgain_ref, invf_ref,
    o_ref, h_ref, qa_ref, ga_ref, kv_ref, *, tiles_per_seq,
):
    tm = x_ref.shape[0]
    first_tile = (pl.program_id(0) % tiles_per_seq) == 0

    @pl.when(first_tile)
    def _():
        kv_ref[0:BLOCK, :] = jnp.zeros((BLOCK, kv_ref.shape[1]), kv_ref.dtype)

    h_ref[...] = _rmsnorm_bf16(x_ref[...], gain_ref[...])

    lane = _lane_iota((1, LANES))
    first_half = (lane % HEAD_DIM) < (HEAD_DIM // 2)
    low = lane < HEAD_DIM
    n_groups = LANES // (HEAD_DIM // 2)
    group = lane // (HEAD_DIM // 2)
    packed_ang = pos_ref[...].astype(jnp.float32) * invf_ref[...]

    def spread(packed):
        rolled = [packed] + [pltpu.roll(packed, s * (HEAD_DIM // 2), 1) for s in range(1, n_groups)]
        quarters = []
        for g in range(n_groups):
            t = rolled[(n_groups - 1 - g) % n_groups]
            for j in reversed(range(n_groups - 1)):
                t = jnp.where(group == j, rolled[(j - g) % n_groups], t)
            quarters.append(t)
        return jnp.concatenate(quarters, axis=0)

    cos = spread(jnp.cos(packed_ang))
    sin_signed = spread(jnp.sin(packed_ang)) * jnp.where(first_half, -1.0, 1.0)

    r = lax.broadcasted_iota(jnp.int32, (LANES, LANES), 0) // HEAD_DIM
    c = lax.broadcasted_iota(jnp.int32, (LANES, LANES), 1) // HEAD_DIM
    head_ones = jnp.where(r == c, 1.0, 0.0).astype(jnp.bfloat16)

    def norm_rope(a, head_gain, scale):
        ss = jnp.dot((a * a).astype(jnp.bfloat16), head_ones, preferred_element_type=jnp.float32)
        y = a * lax.rsqrt(ss * (1.0 / HEAD_DIM) + EPS) * head_gain
        partner = jnp.where(
            first_half, pltpu.roll(y, LANES - HEAD_DIM // 2, 1), pltpu.roll(y, HEAD_DIM // 2, 1)
        )
        y = y * cos + partner * sin_signed
        return y * scale if scale != 1.0 else y

    def project(src_col):
        return jnp.dot(h_ref[...], w_ref[:, src_col : src_col + CHUNK], preferred_element_type=jnp.float32)

    def store(slot, col, val):
        o_ref[:, slot * SLOT + col : slot * SLOT + col + val.shape[1]] = val.astype(o_ref.dtype)

    q_scale = LOG2E / math.sqrt(HEAD_DIM)

    def swa_queries(acc, col):
        for half in range(CHUNK // LANES):
            a = acc[:, half * LANES : (half + 1) * LANES]
            lanes = slice(col + half * LANES, col + (half + 1) * LANES)
            qa_ref[:, lanes] = norm_rope(a, qgain_ref[...], q_scale).astype(qa_ref.dtype)

    def swa_keys_values(acc):
        k = norm_rope(acc[:, :LANES], kgain_ref[...], 1.0)
        v = acc[:, LANES:]
        for base, t in ((0, k), (2 * LANES, v)):
            swapped = pltpu.roll(t, HEAD_DIM, 1)
            kv_ref[BLOCK:, base : base + LANES] = jnp.where(low, t, swapped).astype(kv_ref.dtype)
            kv_ref[BLOCK:, base + LANES : base + 2 * LANES] = jnp.where(low, swapped, t).astype(kv_ref.dtype)

    row2, lane2, stack_heads, unstack_heads = _pair_masks()
    in_cur_block = lane2 <= (row2 & (BLOCK - 1))
    kv_group = [(2 * p) // (SWA_Q_HEADS // SWA_KV_HEADS) for p in range(SWA_PAIRS)]

    pairs_of_group = [[p for p in range(SWA_PAIRS) if kv_group[p] == g] for g in range(SWA_KV_HEADS)]
    rows2 = 2 * BLOCK

    def swa_scores(j):
        scores = [None] * SWA_PAIRS
        for g, pairs in enumerate(pairs_of_group):
            q_all = jnp.concatenate(
                [stack_heads(qa_ref[j * BLOCK : (j + 1) * BLOCK, p * LANES : (p + 1) * LANES]) for p in pairs],
                axis=0,
            )
            k_win = kv_ref[j * BLOCK : (j + 2) * BLOCK, g * LANES : (g + 1) * LANES]
            s_all = lax.dot_general(q_all, k_win, _NT, preferred_element_type=jnp.float32)
            for n, p in enumerate(pairs):
                scores[p] = s_all[n * rows2 : (n + 1) * rows2]
        return scores

    def swa_softmax(j, scores):
        probs = []
        for p in range(SWA_PAIRS):
            s = jnp.where(in_cur_block, scores[p][:, BLOCK:], scores[p][:, :BLOCK])
            if j == 0:
                has_prev_cells = jnp.logical_not((jnp.zeros_like(lane2) + first_tile.astype(jnp.int32)) > 0)
                s = jnp.where(jnp.logical_or(in_cur_block, has_prev_cells), s, -jnp.inf)
            sink = jnp.where(
                row2[:, :1] < BLOCK, LOG2E * sinks_ref[0, 2 * p], LOG2E * sinks_ref[0, 2 * p + 1]
            )
            m = jnp.max(s, axis=-1, keepdims=True)
            e = jnp.exp2(s - m)
            denom = jnp.sum(e, axis=-1, keepdims=True) + jnp.exp2(sink - m)
            e_both = jnp.concatenate(
                [jnp.where(in_cur_block, 0.0, e), jnp.where(in_cur_block, e, 0.0)], axis=1
            ).astype(jnp.bfloat16)
            probs.append((e_both, denom))
        return probs

    def swa_values(j, probs):
        rows = slice(j * BLOCK, (j + 1) * BLOCK)
        for g, pairs in enumerate(pairs_of_group):
            v_cols = slice((SWA_KV_HEADS + g) * LANES, (SWA_KV_HEADS + g + 1) * LANES)
            e_all = jnp.concatenate([probs[p][0] for p in pairs], axis=0)
            o_all = jnp.dot(e_all, kv_ref[j * BLOCK : (j + 2) * BLOCK, v_cols], preferred_element_type=jnp.float32)
            for n, p in enumerate(pairs):
                o2 = o_all[n * rows2 : (n + 1) * rows2]
                gate = ga_ref[rows, p * LANES : (p + 1) * LANES].astype(jnp.float32)
                o_ref[rows, SLOT_YA * SLOT + p * LANES : SLOT_YA * SLOT + (p + 1) * LANES] = (
                    unstack_heads(o2 / probs[p][1]) * gate
                ).astype(o_ref.dtype)

    first, second = 0, CHUNK
    qa_first = project(SRC_QA + first)
    kva = project(SRC_KA)
    qa_second = project(SRC_QA + second)
    ga_ref[:, first : first + CHUNK] = _silu(project(SRC_GA + first)).astype(ga_ref.dtype)
    swa_queries(qa_first, first)
    ga_ref[:, second : second + CHUNK] = _silu(project(SRC_GA + second)).astype(ga_ref.dtype)
    swa_keys_values(kva)
    half_rows = tm // 2
    scale_q = lambda v: v * q_scale
    plain_pieces = [
        (slot, src, col, r0, post)
        for slot, src, post in (
            (SLOT_QB, SRC_QB, scale_q), (SLOT_KB, SRC_KB, None), (SLOT_VB, SRC_VB, None), (SLOT_GB, SRC_GB, _silu)
        )
        for col in range(0, SLOT, CHUNK)
        for r0 in (0, half_rows)
    ]

    def plain_piece(slot, src, col, r0, post):
        val = jnp.dot(
            h_ref[r0 : r0 + half_rows], w_ref[:, src + col : src + col + CHUNK], preferred_element_type=jnp.float32
        )
        val = post(val) if post else val
        o_ref[r0 : r0 + half_rows, slot * SLOT + col : slot * SLOT + col + CHUNK] = val.astype(o_ref.dtype)

    swa_queries(qa_second, second)
    n_blk = tm // BLOCK
    n_slots = 2 * n_blk
    by_slot = [[] for _ in range(n_slots)]
    for k, piece in enumerate(plain_pieces):
        by_slot[k * n_slots // len(plain_pieces)].append(piece)

    scores = swa_scores(0)
    for j in range(n_blk):
        for piece in by_slot[2 * j]:
            plain_piece(*piece)
        probs = swa_softmax(j, scores)
        if j + 1 < n_blk:
            scores = swa_scores(j + 1)
        for piece in by_slot[2 * j + 1]:
            plain_piece(*piece)
        swa_values(j, probs)

    kv_ref[0:BLOCK, :] = kv_ref[tm : tm + BLOCK, :]


def _inproj(sinks, x2, pos2, gain, w_bf16, qgain, kgain, invf, tm, seq_len):
    n, d = x2.shape
    full = lambda i: (0, 0)
    return pl.pallas_call(
        functools.partial(_inproj_kernel, tiles_per_seq=seq_len // tm),
        out_shape=jax.ShapeDtypeStruct((n, ACT_WIDTH), jnp.bfloat16),
        grid=(n // tm,),
        in_specs=[
            pl.BlockSpec(memory_space=pltpu.SMEM),
            pl.BlockSpec((tm, d), lambda i: (i, 0)),
            pl.BlockSpec((tm // (LANES // (HEAD_DIM // 2)), LANES), lambda i: (i, 0)),
            pl.BlockSpec((1, d), full),
            pl.BlockSpec(w_bf16.shape, full),
            pl.BlockSpec((1, LANES), full),
            pl.BlockSpec((1, LANES), full),
            pl.BlockSpec((1, LANES), full),
        ],
        out_specs=pl.BlockSpec((tm, ACT_WIDTH), lambda i: (i, 0)),
        scratch_shapes=[
            pltpu.VMEM((tm, d), jnp.bfloat16),
            pltpu.VMEM((tm, SWA_WIDTH), jnp.bfloat16),
            pltpu.VMEM((tm, SWA_WIDTH), jnp.bfloat16),
            pltpu.VMEM((BLOCK + tm, 2 * SWA_KV_HEADS * LANES), jnp.bfloat16),
        ],
        compiler_params=pltpu.CompilerParams(
            dimension_semantics=("arbitrary",), vmem_limit_bytes=VMEM_LIMIT_BYTES
        ),
        name="inproj",
    )(sinks, x2, pos2, gain, w_bf16, qgain, kgain, invf)


def _mixer_kernel(
    x_ref, ya_ref, qb_ref, gb_ref, kb_ref, vb_ref, wout_ref,
    o_ref, yb_ref, acc_ref, fail_ref, lowest_ref, lowest_smem, *, tq, steps_per_seq,
):
    t = pl.program_id(0)
    last_step = pl.num_programs(0) - 1
    step = t % steps_per_seq
    n_sub = tq // BLOCK
    row2, lane2, stack_heads, unstack_heads = _pair_masks()
    qrow_w = lax.broadcasted_iota(jnp.int32, (2 * BLOCK, CHUNK), 0) & (BLOCK - 1)
    key_w = lax.broadcasted_iota(jnp.int32, (2 * BLOCK, CHUNK), 1)
    causal_bias = jnp.where(lane2 < (row2 & (BLOCK - 1)), 0.0, MASKED_SCORE)
    kr = lax.broadcasted_iota(jnp.int32, (CHUNK, CHUNK), 0)
    kc = lax.broadcasted_iota(jnp.int32, (CHUNK, CHUNK), 1)
    suffix_ones = jnp.where(kr >= kc, 1.0, 0.0).astype(jnp.bfloat16)
    sb_pairs = range(SB_PAIRS)

    def sb_scores(p, q2, ks, mask):
        k_t = kb_ref[0, pl.ds(ks, CHUNK), p * LANES : (p + 1) * LANES]
        z = lax.dot_general(q2, k_t, _NT, preferred_element_type=jnp.float32)
        return mask(z)

    def mask_where(valid):
        return lambda z: jnp.where(valid, z, MASKED_SCORE)

    def mask_second_block_causal(z):
        return jnp.concatenate([z[:, :BLOCK], z[:, BLOCK:] + causal_bias], axis=1)

    def sb_suffixes(zs):
        sp = jnp.concatenate([_softplus_log2(z).astype(jnp.bfloat16) for z in zs], axis=0)
        upto = jnp.dot(sp, suffix_ones, preferred_element_type=jnp.float32)
        rows = 2 * BLOCK
        return [upto[p * rows : (p + 1) * rows] for p in range(len(zs))]

    def sb_accumulate(sb, p, z, upto, ks, first):
        arg = z - upto
        if not first:
            fail = fail_ref[sb, p]
            arg = arg - jnp.concatenate([fail] * (CHUNK // LANES), axis=1)
        w = jnp.exp2(arg).astype(jnp.bfloat16)
        v_t = vb_ref[0, pl.ds(ks, CHUNK), p * LANES : (p + 1) * LANES]
        pv = jnp.dot(w, v_t, preferred_element_type=jnp.float32)
        total = jnp.broadcast_to(upto[:, :1], (2 * BLOCK, LANES))
        if first:
            acc_ref[sb, p] = pv
            fail_ref[sb, p] = total
            return total
        acc_ref[sb, p] += pv
        fail_ref[sb, p] = fail + total
        return fail + total

    def block_index(sb):
        return step * n_sub + sb

    def sb_queries(r0):
        return [stack_heads(qb_ref[0, pl.ds(r0, BLOCK), p * LANES : (p + 1) * LANES]) for p in sb_pairs]

    def stage_scores(sb):
        blk = block_index(sb)
        kp = pl.multiple_of(jnp.maximum(blk - 1, 0) * BLOCK, BLOCK)
        if sb == 0:
            mask = mask_where(key_w < (qrow_w + jnp.where(blk > 0, BLOCK, 0)))
        else:
            mask = mask_second_block_causal
        zs = [sb_scores(p, q2, kp, mask) for p, q2 in enumerate(sb_queries(sb * BLOCK))]
        return dict(blk=blk, kp=kp, zs=zs)

    def stage_suffix(st):
        st["uptos"] = sb_suffixes(st["zs"])

    def stage_values(sb, st):
        lowest = None
        for p in sb_pairs:
            f = sb_accumulate(sb, p, st["zs"][p], st["uptos"][p], st["kp"], first=True)
            lowest = f if lowest is None else jnp.minimum(lowest, f)
        if sb < 2:
            lowest = jnp.where(st["blk"] >= 2, lowest, jnp.inf)
        lowest_ref[sb] = jnp.min(lowest.reshape(-1, SUBLANES, LANES), axis=0)

    def finish_chunk(piece):
        c, r0 = piece
        rows = slice(r0, r0 + tq // 2)
        o_ref[0, rows, c : c + CHUNK] = (
            x_ref[0, rows, c : c + CHUNK]
            + jnp.dot(ya_ref[0, rows], wout_ref[:SWA_WIDTH, c : c + CHUNK], preferred_element_type=jnp.float32)
            + jnp.dot(yb_ref[rows], wout_ref[SWA_WIDTH:, c : c + CHUNK], preferred_element_type=jnp.float32)
        )

    all_chunks = [(c, r0) for c in range(0, o_ref.shape[2], CHUNK) for r0 in (0, tq // 2)]

    def attend(out_chunks):
        def next_out_chunk():
            if out_chunks:
                finish_chunk(out_chunks.pop(0))

        stage = stage_scores(0)
        for sb in range(n_sub):
            next_out_chunk()
            stage_suffix(stage)
            following = stage_scores(sb + 1) if sb + 1 < n_sub else None
            stage_values(sb, stage)
            stage = following
        while out_chunks:
            next_out_chunk()

        pl.when(jnp.min(lowest_ref[...]) < SB_DONE_LOG2)(visit_earlier_tiles)

        for sb in range(n_sub):
            rows = slice(sb * BLOCK, (sb + 1) * BLOCK)
            for p in sb_pairs:
                gate = gb_ref[0, rows, p * LANES : (p + 1) * LANES].astype(jnp.float32)
                yb_ref[rows, p * LANES : (p + 1) * LANES] = (
                    unstack_heads(acc_ref[sb, p]) * gate
                ).astype(yb_ref.dtype)

    def visit_earlier_tiles():
        for sb in range(n_sub):
            lowest_smem[sb] = jnp.min(lowest_ref[sb])

        def earlier_tiles(sb, carry):
            blk = block_index(sb)
            q2s = sb_queries(pl.multiple_of(sb * BLOCK, BLOCK))
            n_tiles = blk // 2

            def cond(c):
                n, lowest = c
                return jnp.logical_and(n < n_tiles, lowest < SB_DONE_LOG2)

            def body(c):
                n, _ = c
                start = (blk - 1) * BLOCK - CHUNK * (n + 1)
                ks = pl.multiple_of(jnp.maximum(start, 0), BLOCK)
                tile_valid = key_w < (CHUNK + jnp.minimum(start, 0))
                zs = [sb_scores(p, q2s[p], ks, mask_where(tile_valid)) for p in sb_pairs]
                uptos = sb_suffixes(zs)
                lowest = None
                for p in sb_pairs:
                    f = sb_accumulate(sb, p, zs[p], uptos[p], ks, first=False)
                    lowest = f if lowest is None else jnp.minimum(lowest, f)
                return n + 1, jnp.min(lowest)

            lax.while_loop(cond, body, (jnp.int32(0), lowest_smem[sb]))
            return carry

        lax.fori_loop(0, n_sub, earlier_tiles, 0)

    @pl.when(t == 0)
    def _():
        yb_ref[...] = jnp.zeros(yb_ref.shape, yb_ref.dtype)

    @pl.when(t < last_step)
    def _():
        attend(list(all_chunks))

    @pl.when(t == last_step)
    def _():
        for c in all_chunks:
            finish_chunk(c)


def _mixer(x, act, wout_bf16, tq):
    b, s, d = x.shape
    steps_per_seq = s // tq
    n_tiles = b * steps_per_seq

    def attended(t, last):
        tile = jnp.minimum(t, n_tiles - 1)
        return tile // steps_per_seq, tile % steps_per_seq, last

    def finished(t, last):
        tile = jnp.maximum(t - 1, 0)
        return tile // steps_per_seq, tile % steps_per_seq, last

    now = lambda slot: pl.BlockSpec((1, tq, SLOT), lambda t, slot=slot: attended(t, slot))
    seq = lambda slot: pl.BlockSpec((1, s, SLOT), lambda t, slot=slot: (attended(t, slot)[0], 0, slot))
    return pl.pallas_call(
        functools.partial(_mixer_kernel, tq=tq, steps_per_seq=steps_per_seq),
        out_shape=jax.ShapeDtypeStruct((b, s, d), jnp.float32),
        grid=(n_tiles + 1,),
        in_specs=[
            pl.BlockSpec((1, tq, d), lambda t: finished(t, 0)),
            pl.BlockSpec((1, tq, SLOT), lambda t: finished(t, SLOT_YA)),
            now(SLOT_QB), now(SLOT_GB),
            seq(SLOT_KB), seq(SLOT_VB),
            pl.BlockSpec(wout_bf16.shape, lambda t: (0, 0)),
        ],
        out_specs=pl.BlockSpec((1, tq, d), lambda t: finished(t, 0)),
        scratch_shapes=[
            pltpu.VMEM((tq, SB_WIDTH), jnp.bfloat16),
            pltpu.VMEM((tq // BLOCK, SB_PAIRS, 2 * BLOCK, LANES), jnp.float32),
            pltpu.VMEM((tq // BLOCK, SB_PAIRS, 2 * BLOCK, LANES), jnp.float32),
            pltpu.VMEM((tq // BLOCK, SUBLANES, LANES), jnp.float32),
            pltpu.SMEM((tq // BLOCK,), jnp.float32),
        ],
        compiler_params=pltpu.CompilerParams(
            dimension_semantics=("arbitrary",), vmem_limit_bytes=VMEM_LIMIT_BYTES
        ),
        name="mixer",
    )(x, act, act, act, act, act, wout_bf16)


def kernel(x, positions, norm_gain, w_in, q_norm_gain, k_norm_gain, sinks, w_out):
    b, s, d = x.shape
    tile = 1024
    assert s % tile == 0 and s >= CHUNK
    depth = w_in.shape[0]
    half = HEAD_DIM // 2
    inv_freq = ROPE_THETA ** (-jnp.arange(half, dtype=jnp.float32) * 2.0 / HEAD_DIM)
    invf = jnp.tile(inv_freq, LANES // half).reshape(1, LANES)
    n_groups = LANES // half
    packed_pos = jnp.repeat(
        positions.reshape(-1, n_groups, tile // n_groups).transpose(0, 2, 1), half, axis=2
    ).reshape(-1, LANES)
    for l in range(depth):
        w_bf16 = w_in[l].astype(jnp.bfloat16)
        gain = norm_gain[l].reshape(1, d)
        act = _inproj(
            sinks[l].reshape(1, SWA_Q_HEADS),
            x.reshape(b * s, d),
            packed_pos,
            gain,
            w_bf16,
            jnp.tile(q_norm_gain[l], LANES // HEAD_DIM).reshape(1, LANES),
            jnp.tile(k_norm_gain[l], LANES // HEAD_DIM).reshape(1, LANES),
            invf,
            tm=tile,
            seq_len=s,
        )
        x = _mixer(x, act.reshape(b, s, ACT_WIDTH), w_out[l].astype(jnp.bfloat16), tq=tile)
    return x
```

```python
import functools
import math

import jax
import jax.numpy as jnp
from jax import lax
from jax.experimental import pallas as pl
from jax.experimental.pallas import tpu as pltpu

HEAD_DIM = 64
SWA_Q_HEADS = 8
SWA_KV_HEADS = 2
SB_HEADS = 8
BLOCK = 128
ROPE_THETA = 10000.0
EPS = 1e-6
LANES = 128
SUBLANES = 8
CHUNK = 256
LOG2E = math.log2(math.e)

SWA_WIDTH = SWA_Q_HEADS * HEAD_DIM
SWA_KV_WIDTH = SWA_KV_HEADS * HEAD_DIM
SB_WIDTH = SB_HEADS * HEAD_DIM
MIX_WIDTH = SWA_WIDTH + SB_WIDTH
SWA_PAIRS = SWA_Q_HEADS // 2
SB_PAIRS = SB_HEADS // 2

SRC_QA = 0
SRC_KA = SRC_QA + SWA_WIDTH
SRC_VA = SRC_KA + SWA_KV_WIDTH
SRC_GA = SRC_VA + SWA_KV_WIDTH
SRC_QB = SRC_GA + SWA_WIDTH
SRC_KB = SRC_QB + SB_WIDTH
SRC_VB = SRC_KB + SB_WIDTH
SRC_GB = SRC_VB + SB_WIDTH

SLOT = 512
SLOT_YA, SLOT_QB, SLOT_KB, SLOT_VB, SLOT_GB = range(5)
ACT_WIDTH = 5 * SLOT

VMEM_LIMIT_BYTES = 56 * 1024 * 1024

SB_DONE_LOG2 = 128.0
MASKED_SCORE = -1e30

_NT = (((1,), (1,)), ((), ()))


def _lane_iota(shape):
    return lax.broadcasted_iota(jnp.int32, shape, len(shape) - 1)


def _rmsnorm_bf16(x, gain):
    ms = jnp.mean(x * x, axis=-1, keepdims=True)
    return (x * lax.rsqrt(ms + EPS) * gain).astype(jnp.bfloat16)


def _silu(g):
    return g * (1.0 / (1.0 + jnp.exp(-g)))


def _softplus_log2(z2):
    return jnp.maximum(z2, 0.0) + jnp.log2(1.0 + jnp.exp2(-jnp.abs(z2)))


def _pair_masks():
    row2 = lax.broadcasted_iota(jnp.int32, (2 * BLOCK, LANES), 0)
    lane2 = lax.broadcasted_iota(jnp.int32, (2 * BLOCK, LANES), 1)
    own_head = (row2 < BLOCK) == (lane2 < HEAD_DIM)
    low = _lane_iota((1, LANES)) < HEAD_DIM

    def stack_heads(q_pair):
        q2 = jnp.concatenate([q_pair, q_pair], axis=0)
        return jnp.where(own_head, q2, jnp.zeros_like(q2))

    def unstack_heads(o2):
        return jnp.where(low, o2[:BLOCK], o2[BLOCK:])

    return row2, lane2, stack_heads, unstack_heads


def _inproj_kernel(
    sinks_ref, x_ref, pos_ref, gain_ref, w_ref, qgain_ref, kgain_ref, invf_ref,
    o_ref, h_ref, qa_ref, ga_ref, kv_ref, vt_ref, *, tiles_per_seq,
):
    tm = x_ref.shape[0]
    first_tile = (pl.program_id(0) % tiles_per_seq) == 0

    @pl.when(first_tile)
    def _():
        kv_ref[0:BLOCK, :] = jnp.zeros((BLOCK, kv_ref.shape[1]), kv_ref.dtype)
        vt_ref[:, :, 0:BLOCK] = jnp.zeros((SWA_KV_HEADS, LANES, BLOCK), vt_ref.dtype)

    h_ref[...] = _rmsnorm_bf16(x_ref[...], gain_ref[...])

    lane = _lane_iota((1, LANES))
    first_half = (lane % HEAD_DIM) < (HEAD_DIM // 2)
    low = lane < HEAD_DIM
    n_groups = LANES // (HEAD_DIM // 2)
    group = lane // (HEAD_DIM // 2)
    packed_ang = pos_ref[...].astype(jnp.float32) * invf_ref[...]

    def spread(packed):
        rolled = [packed] + [pltpu.roll(packed, s * (HEAD_DIM // 2), 1) for s in range(1, n_groups)]
        quarters = []
        for g in range(n_groups):
            t = rolled[(n_groups - 1 - g) % n_groups]
            for j in reversed(range(n_groups - 1)):
                t = jnp.where(group == j, rolled[(j - g) % n_groups], t)
            quarters.append(t)
        return jnp.concatenate(quarters, axis=0)

    cos = spread(jnp.cos(packed_ang))
    sin_signed = spread(jnp.sin(packed_ang)) * jnp.where(first_half, -1.0, 1.0)

    r = lax.broadcasted_iota(jnp.int32, (LANES, LANES), 0) // HEAD_DIM
    c = lax.broadcasted_iota(jnp.int32, (LANES, LANES), 1) // HEAD_DIM
    head_ones = jnp.where(r == c, 1.0, 0.0).astype(jnp.bfloat16)

    def norm_rope(a, head_gain, scale):
        ss = jnp.dot((a * a).astype(jnp.bfloat16), head_ones, preferred_element_type=jnp.float32)
        y = a * lax.rsqrt(ss * (1.0 / HEAD_DIM) + EPS) * head_gain
        partner = jnp.where(
            first_half, pltpu.roll(y, LANES - HEAD_DIM // 2, 1), pltpu.roll(y, HEAD_DIM // 2, 1)
        )
        y = y * cos + partner * sin_signed
        return y * scale if scale != 1.0 else y

    def project(src_col):
        return jnp.dot(h_ref[...], w_ref[:, src_col : src_col + CHUNK], preferred_element_type=jnp.float32)

    def store(slot, col, val):
        o_ref[:, slot * SLOT + col : slot * SLOT + col + val.shape[1]] = val.astype(o_ref.dtype)

    q_scale = LOG2E / math.sqrt(HEAD_DIM)

    def swa_queries(acc, col):
        for half in range(CHUNK // LANES):
            a = acc[:, half * LANES : (half + 1) * LANES]
            lanes = slice(col + half * LANES, col + (half + 1) * LANES)
            qa_ref[:, lanes] = norm_rope(a, qgain_ref[...], q_scale).astype(qa_ref.dtype)

    def swa_keys_values(acc):
        k = norm_rope(acc[:, :LANES], kgain_ref[...], 1.0)
        v = acc[:, LANES:]
        for base, t in ((0, k), (2 * LANES, v)):
            swapped = pltpu.roll(t, HEAD_DIM, 1)
            kv_ref[BLOCK:, base : base + LANES] = jnp.where(low, t, swapped).astype(kv_ref.dtype)
            kv_ref[BLOCK:, base + LANES : base + 2 * LANES] = jnp.where(low, swapped, t).astype(kv_ref.dtype)
        v_t = v.T
        for g in range(SWA_KV_HEADS):
            head_t = v_t[g * HEAD_DIM : (g + 1) * HEAD_DIM]
            vt_ref[g, :, BLOCK:] = jnp.concatenate([head_t, head_t], axis=0).astype(vt_ref.dtype)

    row2, lane2, stack_heads, unstack_heads = _pair_masks()
    in_cur_block = lane2 <= (row2 & (BLOCK - 1))
    kv_group = [(2 * p) // (SWA_Q_HEADS // SWA_KV_HEADS) for p in range(SWA_PAIRS)]

    def swa_scores(j):
        scores = []
        for p in range(SWA_PAIRS):
            q2 = stack_heads(qa_ref[j * BLOCK : (j + 1) * BLOCK, p * LANES : (p + 1) * LANES])
            k_win = kv_ref[j * BLOCK : (j + 2) * BLOCK, kv_group[p] * LANES : (kv_group[p] + 1) * LANES]
            scores.append(lax.dot_general(q2, k_win, _NT, preferred_element_type=jnp.float32))
        return scores

    def swa_softmax(j, scores):
        probs = []
        for p in range(SWA_PAIRS):
            s = jnp.where(in_cur_block, scores[p][:, BLOCK:], scores[p][:, :BLOCK])
            if j == 0:
                has_prev_cells = jnp.logical_not((jnp.zeros_like(lane2) + first_tile.astype(jnp.int32)) > 0)
                s = jnp.where(jnp.logical_or(in_cur_block, has_prev_cells), s, -jnp.inf)
            sink = jnp.where(
                row2[:, :1] < BLOCK, LOG2E * sinks_ref[0, 2 * p], LOG2E * sinks_ref[0, 2 * p + 1]
            )
            m = jnp.max(s, axis=-1, keepdims=True)
            e = jnp.exp2(s - m)
            denom = jnp.sum(e, axis=-1, keepdims=True) + jnp.exp2(sink - m)
            e_both = jnp.concatenate(
                [jnp.where(in_cur_block, 0.0, e), jnp.where(in_cur_block, e, 0.0)], axis=1
            ).astype(jnp.bfloat16)
            probs.append((e_both, denom))
        return probs

    def swa_values(j, probs):
        rows = slice(j * BLOCK, (j + 1) * BLOCK)
        for p in range(SWA_PAIRS):
            e_both, denom = probs[p]
            o_t = lax.dot_general(
                vt_ref[kv_group[p], :, j * BLOCK : (j + 2) * BLOCK], e_both, _NT, preferred_element_type=jnp.float32
            )
            o_pair = jnp.concatenate([o_t[:HEAD_DIM, :BLOCK], o_t[HEAD_DIM:, BLOCK:]], axis=0).T
            scale = jnp.where(low, 1.0 / denom[:BLOCK], 1.0 / denom[BLOCK:])
            gate = ga_ref[rows, p * LANES : (p + 1) * LANES].astype(jnp.float32)
            o_ref[rows, SLOT_YA * SLOT + p * LANES : SLOT_YA * SLOT + (p + 1) * LANES] = (
                o_pair * scale * gate
            ).astype(o_ref.dtype)

    first, second = 0, CHUNK
    qa_first = project(SRC_QA + first)
    kva = project(SRC_KA)
    qa_second = project(SRC_QA + second)
    ga_ref[:, first : first + CHUNK] = _silu(project(SRC_GA + first)).astype(ga_ref.dtype)
    swa_queries(qa_first, first)
    ga_ref[:, second : second + CHUNK] = _silu(project(SRC_GA + second)).astype(ga_ref.dtype)
    swa_keys_values(kva)
    half_rows = tm // 2
    scale_q = lambda v: v * q_scale
    plain_pieces = [
        (slot, src, col, r0, post)
        for slot, src, post in (
            (SLOT_QB, SRC_QB, scale_q), (SLOT_KB, SRC_KB, None), (SLOT_VB, SRC_VB, None), (SLOT_GB, SRC_GB, _silu)
        )
        for col in range(0, SLOT, CHUNK)
        for r0 in (0, half_rows)
    ]

    def plain_piece(slot, src, col, r0, post):
        val = jnp.dot(
            h_ref[r0 : r0 + half_rows], w_ref[:, src + col : src + col + CHUNK], preferred_element_type=jnp.float32
        )
        val = post(val) if post else val
        o_ref[r0 : r0 + half_rows, slot * SLOT + col : slot * SLOT + col + CHUNK] = val.astype(o_ref.dtype)

    swa_queries(qa_second, second)
    n_blk = tm // BLOCK
    n_slots = 2 * n_blk
    by_slot = [[] for _ in range(n_slots)]
    for k, piece in enumerate(plain_pieces):
        by_slot[k * n_slots // len(plain_pieces)].append(piece)

    scores = swa_scores(0)
    for j in range(n_blk):
        for piece in by_slot[2 * j]:
            plain_piece(*piece)
        probs = swa_softmax(j, scores)
        if j + 1 < n_blk:
            scores = swa_scores(j + 1)
        for piece in by_slot[2 * j + 1]:
            plain_piece(*piece)
        swa_values(j, probs)

    kv_ref[0:BLOCK, :] = kv_ref[tm : tm + BLOCK, :]
    vt_ref[:, :, 0:BLOCK] = vt_ref[:, :, tm : tm + BLOCK]


def _inproj(sinks, x2, pos2, gain, w_bf16, qgain, kgain, invf, tm, seq_len):
    n, d = x2.shape
    full = lambda i: (0, 0)
    return pl.pallas_call(
        functools.partial(_inproj_kernel, tiles_per_seq=seq_len // tm),
        out_shape=jax.ShapeDtypeStruct((n, ACT_WIDTH), jnp.bfloat16),
        grid=(n // tm,),
        in_specs=[
            pl.BlockSpec(memory_space=pltpu.SMEM),
            pl.BlockSpec((tm, d), lambda i: (i, 0)),
            pl.BlockSpec((tm // (LANES // (HEAD_DIM // 2)), LANES), lambda i: (i, 0)),
            pl.BlockSpec((1, d), full),
            pl.BlockSpec(w_bf16.shape, full),
            pl.BlockSpec((1, LANES), full),
            pl.BlockSpec((1, LANES), full),
            pl.BlockSpec((1, LANES), full),
        ],
        out_specs=pl.BlockSpec((tm, ACT_WIDTH), lambda i: (i, 0)),
        scratch_shapes=[
            pltpu.VMEM((tm, d), jnp.bfloat16),
            pltpu.VMEM((tm, SWA_WIDTH), jnp.bfloat16),
            pltpu.VMEM((tm, SWA_WIDTH), jnp.bfloat16),
            pltpu.VMEM((BLOCK + tm, 2 * SWA_KV_HEADS * LANES), jnp.bfloat16),
            pltpu.VMEM((SWA_KV_HEADS, LANES, BLOCK + tm), jnp.bfloat16),
        ],
        compiler_params=pltpu.CompilerParams(
            dimension_semantics=("arbitrary",), vmem_limit_bytes=VMEM_LIMIT_BYTES
        ),
        name="inproj",
    )(sinks, x2, pos2, gain, w_bf16, qgain, kgain, invf)


def _mixer_kernel(
    x_ref, ya_ref, qb_ref, gb_ref, kb_ref, vb_ref, wout_ref,
    o_ref, yb_ref, acc_ref, fail_ref, lowest_ref, lowest_smem, *, tq, steps_per_seq,
):
    t = pl.program_id(0)
    last_step = pl.num_programs(0) - 1
    step = t % steps_per_seq
    n_sub = tq // BLOCK
    row2, lane2, stack_heads, unstack_heads = _pair_masks()
    qrow_w = lax.broadcasted_iota(jnp.int32, (2 * BLOCK, CHUNK), 0) & (BLOCK - 1)
    key_w = lax.broadcasted_iota(jnp.int32, (2 * BLOCK, CHUNK), 1)
    causal_bias = jnp.where(lane2 < (row2 & (BLOCK - 1)), 0.0, MASKED_SCORE)
    kr = lax.broadcasted_iota(jnp.int32, (CHUNK, CHUNK), 0)
    kc = lax.broadcasted_iota(jnp.int32, (CHUNK, CHUNK), 1)
    suffix_ones = jnp.where(kr >= kc, 1.0, 0.0).astype(jnp.bfloat16)
    sb_pairs = range(SB_PAIRS)

    def sb_scores(p, q2, ks, mask):
        k_t = kb_ref[0, pl.ds(ks, CHUNK), p * LANES : (p + 1) * LANES]
        z = lax.dot_general(q2, k_t, _NT, preferred_element_type=jnp.float32)
        return mask(z)

    def mask_where(valid):
        return lambda z: jnp.where(valid, z, MASKED_SCORE)

    def mask_second_block_causal(z):
        return jnp.concatenate([z[:, :BLOCK], z[:, BLOCK:] + causal_bias], axis=1)

    def sb_suffix(z):
        return jnp.dot(
            _softplus_log2(z).astype(jnp.bfloat16), suffix_ones, preferred_element_type=jnp.float32
        )

    def sb_accumulate(sb, p, z, upto, ks, first):
        arg = z - upto
        if not first:
            fail = fail_ref[sb, p]
            arg = arg - jnp.concatenate([fail] * (CHUNK // LANES), axis=1)
        w = jnp.exp2(arg).astype(jnp.bfloat16)
        v_t = vb_ref[0, pl.ds(ks, CHUNK), p * LANES : (p + 1) * LANES]
        pv = jnp.dot(w, v_t, preferred_element_type=jnp.float32)
        total = jnp.broadcast_to(upto[:, :1], (2 * BLOCK, LANES))
        if first:
            acc_ref[sb, p] = pv
            fail_ref[sb, p] = total
            return total
        acc_ref[sb, p] += pv
        fail_ref[sb, p] = fail + total
        return fail + total

    def block_index(sb):
        return step * n_sub + sb

    def sb_queries(r0):
        return [stack_heads(qb_ref[0, pl.ds(r0, BLOCK), p * LANES : (p + 1) * LANES]) for p in sb_pairs]

    def stage_scores(sb):
        blk = block_index(sb)
        kp = pl.multiple_of(jnp.maximum(blk - 1, 0) * BLOCK, BLOCK)
        if sb == 0:
            mask = mask_where(key_w < (qrow_w + jnp.where(blk > 0, BLOCK, 0)))
        else:
            mask = mask_second_block_causal
        zs = [sb_scores(p, q2, kp, mask) for p, q2 in enumerate(sb_queries(sb * BLOCK))]
        return dict(blk=blk, kp=kp, zs=zs)

    def stage_suffix(st):
        st["uptos"] = [sb_suffix(z) for z in st["zs"]]

    def stage_values(sb, st):
        lowest = None
        for p in sb_pairs:
            f = sb_accumulate(sb, p, st["zs"][p], st["uptos"][p], st["kp"], first=True)
            lowest = f if lowest is None else jnp.minimum(lowest, f)
        if sb < 2:
            lowest = jnp.where(st["blk"] >= 2, lowest, jnp.inf)
        lowest_ref[sb] = jnp.min(lowest.reshape(-1, SUBLANES, LANES), axis=0)

    def finish_chunk(piece):
        c, r0 = piece
        rows = slice(r0, r0 + tq // 2)
        o_ref[0, rows, c : c + CHUNK] = (
            x_ref[0, rows, c : c + CHUNK]
            + jnp.dot(ya_ref[0, rows], wout_ref[:SWA_WIDTH, c : c + CHUNK], preferred_element_type=jnp.float32)
            + jnp.dot(yb_ref[rows], wout_ref[SWA_WIDTH:, c : c + CHUNK], preferred_element_type=jnp.float32)
        )

    all_chunks = [(c, r0) for c in range(0, o_ref.shape[2], CHUNK) for r0 in (0, tq // 2)]

    def attend(out_chunks):
        def next_out_chunk():
            if out_chunks:
                finish_chunk(out_chunks.pop(0))

        stage = stage_scores(0)
        for sb in range(n_sub):
            next_out_chunk()
            stage_suffix(stage)
            following = stage_scores(sb + 1) if sb + 1 < n_sub else None
            stage_values(sb, stage)
            stage = following
        while out_chunks:
            next_out_chunk()

        pl.when(jnp.min(lowest_ref[...]) < SB_DONE_LOG2)(visit_earlier_tiles)

        for sb in range(n_sub):
            rows = slice(sb * BLOCK, (sb + 1) * BLOCK)
            for p in sb_pairs:
                gate = gb_ref[0, rows, p * LANES : (p + 1) * LANES].astype(jnp.float32)
                yb_ref[rows, p * LANES : (p + 1) * LANES] = (
                    unstack_heads(acc_ref[sb, p]) * gate
                ).astype(yb_ref.dtype)

    def visit_earlier_tiles():
        for sb in range(n_sub):
            lowest_smem[sb] = jnp.min(lowest_ref[sb])

        def earlier_tiles(sb, carry):
            blk = block_index(sb)
            q2s = sb_queries(pl.multiple_of(sb * BLOCK, BLOCK))
            n_tiles = blk // 2

            def cond(c):
                n, lowest = c
                return jnp.logical_and(n < n_tiles, lowest < SB_DONE_LOG2)

            def body(c):
                n, _ = c
                start = (blk - 1) * BLOCK - CHUNK * (n + 1)
                ks = pl.multiple_of(jnp.maximum(start, 0), BLOCK)
                tile_valid = key_w < (CHUNK + jnp.minimum(start, 0))
                zs = [sb_scores(p, q2s[p], ks, mask_where(tile_valid)) for p in sb_pairs]
                uptos = [sb_suffix(z) for z in zs]
                lowest = None
                for p in sb_pairs:
                    f = sb_accumulate(sb, p, zs[p], uptos[p], ks, first=False)
                    lowest = f if lowest is None else jnp.minimum(lowest, f)
                return n + 1, jnp.min(lowest)

            lax.while_loop(cond, body, (jnp.int32(0), lowest_smem[sb]))
            return carry

        lax.fori_loop(0, n_sub, earlier_tiles, 0)

    @pl.when(t == 0)
    def _():
        yb_ref[...] = jnp.zeros(yb_ref.shape, yb_ref.dtype)

    @pl.when(t < last_step)
    def _():
        attend(list(all_chunks))

    @pl.when(t == last_step)
    def _():
        for c in all_chunks:
            finish_chunk(c)


def _mixer(x, act, wout_bf16, tq):
    b, s, d = x.shape
    steps_per_seq = s // tq
    n_tiles = b * steps_per_seq

    def attended(t, last):
        tile = jnp.minimum(t, n_tiles - 1)
        return tile // steps_per_seq, tile % steps_per_seq, last

    def finished(t, last):
        tile = jnp.maximum(t - 1, 0)
        return tile // steps_per_seq, tile % steps_per_seq, last

    now = lambda slot: pl.BlockSpec((1, tq, SLOT), lambda t, slot=slot: attended(t, slot))
    seq = lambda slot: pl.BlockSpec((1, s, SLOT), lambda t, slot=slot: (attended(t, slot)[0], 0, slot))
    return pl.pallas_call(
        functools.partial(_mixer_kernel, tq=tq, steps_per_seq=steps_per_seq),
        out_shape=jax.ShapeDtypeStruct((b, s, d), jnp.float32),
        grid=(n_tiles + 1,),
        in_specs=[
            pl.BlockSpec((1, tq, d), lambda t: finished(t, 0)),
            pl.BlockSpec((1, tq, SLOT), lambda t: finished(t, SLOT_YA)),
            now(SLOT_QB), now(SLOT_GB),
            seq(SLOT_KB), seq(SLOT_VB),
            pl.BlockSpec(wout_bf16.shape, lambda t: (0, 0)),
        ],
        out_specs=pl.BlockSpec((1, tq, d), lambda t: finished(t, 0)),
        scratch_shapes=[
            pltpu.VMEM((tq, SB_WIDTH), jnp.bfloat16),
            pltpu.VMEM((tq // BLOCK, SB_PAIRS, 2 * BLOCK, LANES), jnp.float32),
            pltpu.VMEM((tq // BLOCK, SB_PAIRS, 2 * BLOCK, LANES), jnp.float32),
            pltpu.VMEM((tq // BLOCK, SUBLANES, LANES), jnp.float32),
            pltpu.SMEM((tq // BLOCK,), jnp.float32),
        ],
        compiler_params=pltpu.CompilerParams(
            dimension_semantics=("arbitrary",), vmem_limit_bytes=VMEM_LIMIT_BYTES
        ),
        name="mixer",
    )(x, act, act, act, act, act, wout_bf16)


def kernel(x, positions, norm_gain, w_in, q_norm_gain, k_norm_gain, sinks, w_out):
    b, s, d = x.shape
    tile = 1024
    assert s % tile == 0 and s >= CHUNK
    depth = w_in.shape[0]
    half = HEAD_DIM // 2
    inv_freq = ROPE_THETA ** (-jnp.arange(half, dtype=jnp.float32) * 2.0 / HEAD_DIM)
    invf = jnp.tile(inv_freq, LANES // half).reshape(1, LANES)
    n_groups = LANES // half
    packed_pos = jnp.repeat(
        positions.reshape(-1, n_groups, tile // n_groups).transpose(0, 2, 1), half, axis=2
    ).reshape(-1, LANES)
    for l in range(depth):
        w_bf16 = w_in[l].astype(jnp.bfloat16)
        gain = norm_gain[l].reshape(1, d)
        act = _inproj(
            sinks[l].reshape(1, SWA_Q_HEADS),
            x.reshape(b * s, d),
            packed_pos,
            gain,
            w_bf16,
            jnp.tile(q_norm_gain[l], LANES // HEAD_DIM).reshape(1, LANES),
            jnp.tile(k_norm_gain[l], LANES // HEAD_DIM).reshape(1, LANES),
            invf,
            tm=tile,
            seq_len=s,
        )
        x = _mixer(x, act.reshape(b, s, ACT_WIDTH), w_out[l].astype(jnp.bfloat16), tq=tile)
    return x
```

```python
import functools
import math

import jax
import jax.numpy as jnp
from jax import lax
from jax.experimental import pallas as pl
from jax.experimental.pallas import tpu as pltpu

HEAD_DIM = 64
SWA_Q_HEADS = 8
SWA_KV_HEADS = 2
SB_HEADS = 8
BLOCK = 128
ROPE_THETA = 10000.0
EPS = 1e-6
LANES = 128
SUBLANES = 8
CHUNK = 256
LOG2E = math.log2(math.e)

SWA_WIDTH = SWA_Q_HEADS * HEAD_DIM
SWA_KV_WIDTH = SWA_KV_HEADS * HEAD_DIM
SB_WIDTH = SB_HEADS * HEAD_DIM
MIX_WIDTH = SWA_WIDTH + SB_WIDTH
SWA_PAIRS = SWA_Q_HEADS // 2
SB_PAIRS = SB_HEADS // 2

SRC_QA = 0
SRC_KA = SRC_QA + SWA_WIDTH
SRC_VA = SRC_KA + SWA_KV_WIDTH
SRC_GA = SRC_VA + SWA_KV_WIDTH
SRC_QB = SRC_GA + SWA_WIDTH
SRC_KB = SRC_QB + SB_WIDTH
SRC_VB = SRC_KB + SB_WIDTH
SRC_GB = SRC_VB + SB_WIDTH

SLOT = 512
SLOT_YA, SLOT_QB, SLOT_KB, SLOT_VB, SLOT_GB = range(5)
ACT_WIDTH = 5 * SLOT

VMEM_LIMIT_BYTES = 56 * 1024 * 1024

SB_DONE_LOG2 = 128.0
MASKED_SCORE = -1e30

_NT = (((1,), (1,)), ((), ()))


def _lane_iota(shape):
    return lax.broadcasted_iota(jnp.int32, shape, len(shape) - 1)


def _rmsnorm_bf16(x, gain):
    ms = jnp.mean(x * x, axis=-1, keepdims=True)
    return (x * lax.rsqrt(ms + EPS) * gain).astype(jnp.bfloat16)


def _silu(g):
    return g * (1.0 / (1.0 + jnp.exp(-g)))


def _softplus_log2(z2):
    return jnp.maximum(z2, 0.0) + jnp.log2(1.0 + jnp.exp2(-jnp.abs(z2)))


def _pair_masks():
    row2 = lax.broadcasted_iota(jnp.int32, (2 * BLOCK, LANES), 0)
    lane2 = lax.broadcasted_iota(jnp.int32, (2 * BLOCK, LANES), 1)
    own_head = (row2 < BLOCK) == (lane2 < HEAD_DIM)
    low = _lane_iota((1, LANES)) < HEAD_DIM

    def stack_heads(q_pair):
        q2 = jnp.concatenate([q_pair, q_pair], axis=0)
        return jnp.where(own_head, q2, jnp.zeros_like(q2))

    def unstack_heads(o2):
        return jnp.where(low, o2[:BLOCK], o2[BLOCK:])

    return row2, lane2, stack_heads, unstack_heads


def _inproj_kernel(
    sinks_ref, x_ref, pos_ref, gain_ref, w_ref, qgain_ref, kgain_ref, invf_ref,
    o_ref, h_ref, qa_ref, ga_ref, kv_ref, vt_ref, *, tiles_per_seq,
):
    tm = x_ref.shape[0]
    first_tile = (pl.program_id(0) % tiles_per_seq) == 0

    @pl.when(first_tile)
    def _():
        kv_ref[0:BLOCK, :] = jnp.zeros((BLOCK, kv_ref.shape[1]), kv_ref.dtype)
        vt_ref[:, :, 0:BLOCK] = jnp.zeros((SWA_KV_HEADS, LANES, BLOCK), vt_ref.dtype)

    h_ref[...] = _rmsnorm_bf16(x_ref[...], gain_ref[...])

    lane = _lane_iota((1, LANES))
    first_half = (lane % HEAD_DIM) < (HEAD_DIM // 2)
    low = lane < HEAD_DIM
    n_groups = LANES // (HEAD_DIM // 2)
    group = lane // (HEAD_DIM // 2)
    packed_ang = pos_ref[...].astype(jnp.float32) * invf_ref[...]

    def spread(packed):
        rolled = [packed] + [pltpu.roll(packed, s * (HEAD_DIM // 2), 1) for s in range(1, n_groups)]
        quarters = []
        for g in range(n_groups):
            t = rolled[(n_groups - 1 - g) % n_groups]
            for j in reversed(range(n_groups - 1)):
                t = jnp.where(group == j, rolled[(j - g) % n_groups], t)
            quarters.append(t)
        return jnp.concatenate(quarters, axis=0)

    cos = spread(jnp.cos(packed_ang))
    sin_signed = spread(jnp.sin(packed_ang)) * jnp.where(first_half, -1.0, 1.0)

    r = lax.broadcasted_iota(jnp.int32, (LANES, LANES), 0) // HEAD_DIM
    c = lax.broadcasted_iota(jnp.int32, (LANES, LANES), 1) // HEAD_DIM
    head_ones = jnp.where(r == c, 1.0, 0.0).astype(jnp.bfloat16)

    def norm_rope(a, head_gain, scale):
        ss = jnp.dot((a * a).astype(jnp.bfloat16), head_ones, preferred_element_type=jnp.float32)
        y = a * lax.rsqrt(ss * (1.0 / HEAD_DIM) + EPS) * head_gain
        partner = jnp.where(
            first_half, pltpu.roll(y, LANES - HEAD_DIM // 2, 1), pltpu.roll(y, HEAD_DIM // 2, 1)
        )
        y = y * cos + partner * sin_signed
        return y * scale if scale != 1.0 else y

    def project(src_col):
        return jnp.dot(h_ref[...], w_ref[:, src_col : src_col + CHUNK], preferred_element_type=jnp.float32)

    def store(slot, col, val):
        o_ref[:, slot * SLOT + col : slot * SLOT + col + val.shape[1]] = val.astype(o_ref.dtype)

    q_scale = LOG2E / math.sqrt(HEAD_DIM)

    def swa_queries(acc, col):
        for half in range(CHUNK // LANES):
            a = acc[:, half * LANES : (half + 1) * LANES]
            lanes = slice(col + half * LANES, col + (half + 1) * LANES)
            qa_ref[:, lanes] = norm_rope(a, qgain_ref[...], q_scale).astype(qa_ref.dtype)

    def swa_keys_values(acc):
        k = norm_rope(acc[:, :LANES], kgain_ref[...], 1.0)
        swapped = pltpu.roll(k, HEAD_DIM, 1)
        kv_ref[BLOCK:, :LANES] = jnp.where(low, k, swapped).astype(kv_ref.dtype)
        kv_ref[BLOCK:, LANES:] = jnp.where(low, swapped, k).astype(kv_ref.dtype)
        v_t = acc[:, LANES:].T
        for g in range(SWA_KV_HEADS):
            head_t = v_t[g * HEAD_DIM : (g + 1) * HEAD_DIM]
            vt_ref[g, :, BLOCK:] = jnp.concatenate([head_t, head_t], axis=0).astype(vt_ref.dtype)

    row2, lane2, stack_heads, unstack_heads = _pair_masks()
    in_cur_block = lane2 <= (row2 & (BLOCK - 1))
    kv_group = [(2 * p) // (SWA_Q_HEADS // SWA_KV_HEADS) for p in range(SWA_PAIRS)]

    def swa_scores(j):
        scores = []
        for p in range(SWA_PAIRS):
            q2 = stack_heads(qa_ref[j * BLOCK : (j + 1) * BLOCK, p * LANES : (p + 1) * LANES])
            k_win = kv_ref[j * BLOCK : (j + 2) * BLOCK, kv_group[p] * LANES : (kv_group[p] + 1) * LANES]
            scores.append(lax.dot_general(q2, k_win, _NT, preferred_element_type=jnp.float32))
        return scores

    def swa_softmax(j, scores):
        probs = []
        for p in range(SWA_PAIRS):
            s = jnp.where(in_cur_block, scores[p][:, BLOCK:], scores[p][:, :BLOCK])
            if j == 0:
                has_prev_cells = jnp.logical_not((jnp.zeros_like(lane2) + first_tile.astype(jnp.int32)) > 0)
                s = jnp.where(jnp.logical_or(in_cur_block, has_prev_cells), s, -jnp.inf)
            sink = jnp.where(
                row2[:, :1] < BLOCK, LOG2E * sinks_ref[0, 2 * p], LOG2E * sinks_ref[0, 2 * p + 1]
            )
            m = jnp.max(s, axis=-1, keepdims=True)
            e = jnp.exp2(s - m)
            denom = jnp.sum(e, axis=-1, keepdims=True) + jnp.exp2(sink - m)
            e_both = jnp.concatenate(
                [jnp.where(in_cur_block, 0.0, e), jnp.where(in_cur_block, e, 0.0)], axis=1
            ).astype(jnp.bfloat16)
            probs.append((e_both, denom))
        return probs

    def swa_values(j, probs):
        rows = slice(j * BLOCK, (j + 1) * BLOCK)
        for p in range(SWA_PAIRS):
            e_both, denom = probs[p]
            o_t = lax.dot_general(
                vt_ref[kv_group[p], :, j * BLOCK : (j + 2) * BLOCK], e_both, _NT, preferred_element_type=jnp.float32
            )
            o_pair = jnp.concatenate([o_t[:HEAD_DIM, :BLOCK], o_t[HEAD_DIM:, BLOCK:]], axis=0).T
            scale = jnp.where(low, 1.0 / denom[:BLOCK], 1.0 / denom[BLOCK:])
            gate = ga_ref[rows, p * LANES : (p + 1) * LANES].astype(jnp.float32)
            o_ref[rows, SLOT_YA * SLOT + p * LANES : SLOT_YA * SLOT + (p + 1) * LANES] = (
                o_pair * scale * gate
            ).astype(o_ref.dtype)

    first, second = 0, CHUNK
    qa_first = project(SRC_QA + first)
    kva = project(SRC_KA)
    qa_second = project(SRC_QA + second)
    ga_ref[:, first : first + CHUNK] = _silu(project(SRC_GA + first)).astype(ga_ref.dtype)
    swa_queries(qa_first, first)
    ga_ref[:, second : second + CHUNK] = _silu(project(SRC_GA + second)).astype(ga_ref.dtype)
    swa_keys_values(kva)
    half_rows = tm // 2
    scale_q = lambda v: v * q_scale
    plain_pieces = [
        (slot, src, col, r0, post)
        for slot, src, post in (
            (SLOT_QB, SRC_QB, scale_q), (SLOT_KB, SRC_KB, None), (SLOT_VB, SRC_VB, None), (SLOT_GB, SRC_GB, _silu)
        )
        for col in range(0, SLOT, CHUNK)
        for r0 in (0, half_rows)
    ]

    def plain_piece(slot, src, col, r0, post):
        val = jnp.dot(
            h_ref[r0 : r0 + half_rows], w_ref[:, src + col : src + col + CHUNK], preferred_element_type=jnp.float32
        )
        val = post(val) if post else val
        o_ref[r0 : r0 + half_rows, slot * SLOT + col : slot * SLOT + col + CHUNK] = val.astype(o_ref.dtype)

    swa_queries(qa_second, second)
    n_blk = tm // BLOCK
    n_slots = 2 * n_blk
    by_slot = [[] for _ in range(n_slots)]
    for k, piece in enumerate(plain_pieces):
        by_slot[k * n_slots // len(plain_pieces)].append(piece)

    scores = swa_scores(0)
    for j in range(n_blk):
        for piece in by_slot[2 * j]:
            plain_piece(*piece)
        probs = swa_softmax(j, scores)
        if j + 1 < n_blk:
            scores = swa_scores(j + 1)
        for piece in by_slot[2 * j + 1]:
            plain_piece(*piece)
        swa_values(j, probs)

    kv_ref[0:BLOCK, :] = kv_ref[tm : tm + BLOCK, :]
    vt_ref[:, :, 0:BLOCK] = vt_ref[:, :, tm : tm + BLOCK]


def _inproj(sinks, x2, pos2, gain, w_bf16, qgain, kgain, invf, tm, seq_len):
    n, d = x2.shape
    full = lambda i: (0, 0)
    return pl.pallas_call(
        functools.partial(_inproj_kernel, tiles_per_seq=seq_len // tm),
        out_shape=jax.ShapeDtypeStruct((n, ACT_WIDTH), jnp.bfloat16),
        grid=(n // tm,),
        in_specs=[
            pl.BlockSpec(memory_space=pltpu.SMEM),
            pl.BlockSpec((tm, d), lambda i: (i, 0)),
            pl.BlockSpec((tm // (LANES // (HEAD_DIM // 2)), LANES), lambda i: (i, 0)),
            pl.BlockSpec((1, d), full),
            pl.BlockSpec(w_bf16.shape, full),
            pl.BlockSpec((1, LANES), full),
            pl.BlockSpec((1, LANES), full),
            pl.BlockSpec((1, LANES), full),
        ],
        out_specs=pl.BlockSpec((tm, ACT_WIDTH), lambda i: (i, 0)),
        scratch_shapes=[
            pltpu.VMEM((tm, d), jnp.bfloat16),
            pltpu.VMEM((tm, SWA_WIDTH), jnp.bfloat16),
            pltpu.VMEM((tm, SWA_WIDTH), jnp.bfloat16),
            pltpu.VMEM((BLOCK + tm, SWA_KV_HEADS * LANES), jnp.bfloat16),
            pltpu.VMEM((SWA_KV_HEADS, LANES, BLOCK + tm), jnp.bfloat16),
        ],
        compiler_params=pltpu.CompilerParams(
            dimension_semantics=("arbitrary",), vmem_limit_bytes=VMEM_LIMIT_BYTES
        ),
        name="inproj",
    )(sinks, x2, pos2, gain, w_bf16, qgain, kgain, invf)


def _mixer_kernel(
    x_ref, ya_ref, qb_ref, gb_ref, kb_ref, vb_ref, wout_ref,
    o_ref, yb_ref, acc_ref, fail_ref, lowest_ref, lowest_smem, *, tq, steps_per_seq,
):
    t = pl.program_id(0)
    last_step = pl.num_programs(0) - 1
    step = t % steps_per_seq
    n_sub = tq // BLOCK
    row2, lane2, stack_heads, unstack_heads = _pair_masks()
    qrow_w = lax.broadcasted_iota(jnp.int32, (2 * BLOCK, CHUNK), 0) & (BLOCK - 1)
    key_w = lax.broadcasted_iota(jnp.int32, (2 * BLOCK, CHUNK), 1)
    causal_bias = jnp.where(lane2 < (row2 & (BLOCK - 1)), 0.0, MASKED_SCORE)
    kr = lax.broadcasted_iota(jnp.int32, (CHUNK, CHUNK), 0)
    kc = lax.broadcasted_iota(jnp.int32, (CHUNK, CHUNK), 1)
    suffix_ones = jnp.where(kr >= kc, 1.0, 0.0).astype(jnp.bfloat16)
    sb_pairs = range(SB_PAIRS)

    def sb_scores(p, q2, ks, mask):
        k_t = kb_ref[0, pl.ds(ks, CHUNK), p * LANES : (p + 1) * LANES]
        z = lax.dot_general(q2, k_t, _NT, preferred_element_type=jnp.float32)
        return mask(z)

    def mask_where(valid):
        return lambda z: jnp.where(valid, z, MASKED_SCORE)

    def mask_second_block_causal(z):
        return jnp.concatenate([z[:, :BLOCK], z[:, BLOCK:] + causal_bias], axis=1)

    def sb_suffix(z):
        return jnp.dot(
            _softplus_log2(z).astype(jnp.bfloat16), suffix_ones, preferred_element_type=jnp.float32
        )

    def sb_accumulate(sb, p, z, upto, ks, first):
        arg = z - upto
        if not first:
            fail = fail_ref[sb, p]
            arg = arg - jnp.concatenate([fail] * (CHUNK // LANES), axis=1)
        w = jnp.exp2(arg).astype(jnp.bfloat16)
        v_t = vb_ref[0, pl.ds(ks, CHUNK), p * LANES : (p + 1) * LANES]
        pv = jnp.dot(w, v_t, preferred_element_type=jnp.float32)
        total = jnp.broadcast_to(upto[:, :1], (2 * BLOCK, LANES))
        if first:
            acc_ref[sb, p] = pv
            fail_ref[sb, p] = total
            return total
        acc_ref[sb, p] += pv
        fail_ref[sb, p] = fail + total
        return fail + total

    def block_index(sb):
        return step * n_sub + sb

    def sb_queries(r0):
        return [stack_heads(qb_ref[0, pl.ds(r0, BLOCK), p * LANES : (p + 1) * LANES]) for p in sb_pairs]

    def stage_scores(sb):
        blk = block_index(sb)
        kp = pl.multiple_of(jnp.maximum(blk - 1, 0) * BLOCK, BLOCK)
        if sb == 0:
            mask = mask_where(key_w < (qrow_w + jnp.where(blk > 0, BLOCK, 0)))
        else:
            mask = mask_second_block_causal
        zs = [sb_scores(p, q2, kp, mask) for p, q2 in enumerate(sb_queries(sb * BLOCK))]
        return dict(blk=blk, kp=kp, zs=zs)

    def stage_suffix(st):
        st["uptos"] = [sb_suffix(z) for z in st["zs"]]

    def stage_values(sb, st):
        lowest = None
        for p in sb_pairs:
            f = sb_accumulate(sb, p, st["zs"][p], st["uptos"][p], st["kp"], first=True)
            lowest = f if lowest is None else jnp.minimum(lowest, f)
        if sb < 2:
            lowest = jnp.where(st["blk"] >= 2, lowest, jnp.inf)
        lowest_ref[sb] = jnp.min(lowest.reshape(-1, SUBLANES, LANES), axis=0)

    def finish_chunk(piece):
        c, r0 = piece
        rows = slice(r0, r0 + tq // 2)
        o_ref[0, rows, c : c + CHUNK] = (
            x_ref[0, rows, c : c + CHUNK]
            + jnp.dot(ya_ref[0, rows], wout_ref[:SWA_WIDTH, c : c + CHUNK], preferred_element_type=jnp.float32)
            + jnp.dot(yb_ref[rows], wout_ref[SWA_WIDTH:, c : c + CHUNK], preferred_element_type=jnp.float32)
        )

    all_chunks = [(c, r0) for c in range(0, o_ref.shape[2], CHUNK) for r0 in (0, tq // 2)]

    def attend(out_chunks):
        def next_out_chunk():
            if out_chunks:
                finish_chunk(out_chunks.pop(0))

        stage = stage_scores(0)
        for sb in range(n_sub):
            next_out_chunk()
            stage_suffix(stage)
            following = stage_scores(sb + 1) if sb + 1 < n_sub else None
            stage_values(sb, stage)
            stage = following
        while out_chunks:
            next_out_chunk()

        pl.when(jnp.min(lowest_ref[...]) < SB_DONE_LOG2)(visit_earlier_tiles)

        for sb in range(n_sub):
            rows = slice(sb * BLOCK, (sb + 1) * BLOCK)
            for p in sb_pairs:
                gate = gb_ref[0, rows, p * LANES : (p + 1) * LANES].astype(jnp.float32)
                yb_ref[rows, p * LANES : (p + 1) * LANES] = (
                    unstack_heads(acc_ref[sb, p]) * gate
                ).astype(yb_ref.dtype)

    def visit_earlier_tiles():
        for sb in range(n_sub):
            lowest_smem[sb] = jnp.min(lowest_ref[sb])

        def earlier_tiles(sb, carry):
            blk = block_index(sb)
            q2s = sb_queries(pl.multiple_of(sb * BLOCK, BLOCK))
            n_tiles = blk // 2

            def cond(c):
                n, lowest = c
                return jnp.logical_and(n < n_tiles, lowest < SB_DONE_LOG2)

            def body(c):
                n, _ = c
                start = (blk - 1) * BLOCK - CHUNK * (n + 1)
                ks = pl.multiple_of(jnp.maximum(start, 0), BLOCK)
                tile_valid = key_w < (CHUNK + jnp.minimum(start, 0))
                zs = [sb_scores(p, q2s[p], ks, mask_where(tile_valid)) for p in sb_pairs]
                uptos = [sb_suffix(z) for z in zs]
                lowest = None
                for p in sb_pairs:
                    f = sb_accumulate(sb, p, zs[p], uptos[p], ks, first=False)
                    lowest = f if lowest is None else jnp.minimum(lowest, f)
                return n + 1, jnp.min(lowest)

            lax.while_loop(cond, body, (jnp.int32(0), lowest_smem[sb]))
            return carry

        lax.fori_loop(0, n_sub, earlier_tiles, 0)

    @pl.when(t == 0)
    def _():
        yb_ref[...] = jnp.zeros(yb_ref.shape, yb_ref.dtype)

    @pl.when(t < last_step)
    def _():
        attend(list(all_chunks))

    @pl.when(t == last_step)
    def _():
        for c in all_chunks:
            finish_chunk(c)


def _mixer(x, act, wout_bf16, tq):
    b, s, d = x.shape
    steps_per_seq = s // tq
    n_tiles = b * steps_per_seq

    def attended(t, last):
        tile = jnp.minimum(t, n_tiles - 1)
        return tile // steps_per_seq, tile % steps_per_seq, last

    def finished(t, last):
        tile = jnp.maximum(t - 1, 0)
        return tile // steps_per_seq, tile % steps_per_seq, last

    now = lambda slot: pl.BlockSpec((1, tq, SLOT), lambda t, slot=slot: attended(t, slot))
    seq = lambda slot: pl.BlockSpec((1, s, SLOT), lambda t, slot=slot: (attended(t, slot)[0], 0, slot))
    return pl.pallas_call(
        functools.partial(_mixer_kernel, tq=tq, steps_per_seq=steps_per_seq),
        out_shape=jax.ShapeDtypeStruct((b, s, d), jnp.float32),
        grid=(n_tiles + 1,),
        in_specs=[
            pl.BlockSpec((1, tq, d), lambda t: finished(t, 0)),
            pl.BlockSpec((1, tq, SLOT), lambda t: finished(t, SLOT_YA)),
            now(SLOT_QB), now(SLOT_GB),
            seq(SLOT_KB), seq(SLOT_VB),
            pl.BlockSpec(wout_bf16.shape, lambda t: (0, 0)),
        ],
        out_specs=pl.BlockSpec((1, tq, d), lambda t: finished(t, 0)),
        scratch_shapes=[
            pltpu.VMEM((tq, SB_WIDTH), jnp.bfloat16),
            pltpu.VMEM((tq // BLOCK, SB_PAIRS, 2 * BLOCK, LANES), jnp.float32),
            pltpu.VMEM((tq // BLOCK, SB_PAIRS, 2 * BLOCK, LANES), jnp.float32),
            pltpu.VMEM((tq // BLOCK, SUBLANES, LANES), jnp.float32),
            pltpu.SMEM((tq // BLOCK,), jnp.float32),
        ],
        compiler_params=pltpu.CompilerParams(
            dimension_semantics=("arbitrary",), vmem_limit_bytes=VMEM_LIMIT_BYTES
        ),
        name="mixer",
    )(x, act, act, act, act, act, wout_bf16)


def kernel(x, positions, norm_gain, w_in, q_norm_gain, k_norm_gain, sinks, w_out):
    b, s, d = x.shape
    tile = 1024
    assert s % tile == 0 and s >= CHUNK
    depth = w_in.shape[0]
    half = HEAD_DIM // 2
    inv_freq = ROPE_THETA ** (-jnp.arange(half, dtype=jnp.float32) * 2.0 / HEAD_DIM)
    invf = jnp.tile(inv_freq, LANES // half).reshape(1, LANES)
    n_groups = LANES // half
    packed_pos = jnp.repeat(
        positions.reshape(-1, n_groups, tile // n_groups).transpose(0, 2, 1), half, axis=2
    ).reshape(-1, LANES)
    for l in range(depth):
        w_bf16 = w_in[l].astype(jnp.bfloat16)
        gain = norm_gain[l].reshape(1, d)
        act = _inproj(
            sinks[l].reshape(1, SWA_Q_HEADS),
            x.reshape(b * s, d),
            packed_pos,
            gain,
            w_bf16,
            jnp.tile(q_norm_gain[l], LANES // HEAD_DIM).reshape(1, LANES),
            jnp.tile(k_norm_gain[l], LANES // HEAD_DIM).reshape(1, LANES),
            invf,
            tm=tile,
            seq_len=s,
        )
        x = _mixer(x, act.reshape(b, s, ACT_WIDTH), w_out[l].astype(jnp.bfloat16), tq=tile)
    return x
```

```python
import functools
import math

import jax
import jax.numpy as jnp
from jax import lax
from jax.experimental import pallas as pl
from jax.experimental.pallas import tpu as pltpu

HEAD_DIM = 64
SWA_Q_HEADS = 8
SWA_KV_HEADS = 2
SB_HEADS = 8
BLOCK = 128
ROPE_THETA = 10000.0
EPS = 1e-6
LANES = 128
SUBLANES = 8
CHUNK = 256
LOG2E = math.log2(math.e)

SWA_WIDTH = SWA_Q_HEADS * HEAD_DIM
SWA_KV_WIDTH = SWA_KV_HEADS * HEAD_DIM
SB_WIDTH = SB_HEADS * HEAD_DIM
MIX_WIDTH = SWA_WIDTH + SB_WIDTH
SWA_PAIRS = SWA_Q_HEADS // 2
SB_PAIRS = SB_HEADS // 2

SRC_QA = 0
SRC_KA = SRC_QA + SWA_WIDTH
SRC_VA = SRC_KA + SWA_KV_WIDTH
SRC_GA = SRC_VA + SWA_KV_WIDTH
SRC_QB = SRC_GA + SWA_WIDTH
SRC_KB = SRC_QB + SB_WIDTH
SRC_VB = SRC_KB + SB_WIDTH
SRC_GB = SRC_VB + SB_WIDTH

SLOT = 512
SLOT_YA, SLOT_QB, SLOT_KB, SLOT_VB, SLOT_GB = range(5)
ACT_WIDTH = 5 * SLOT

VMEM_LIMIT_BYTES = 56 * 1024 * 1024

SB_DONE_LOG2 = 128.0
MASKED_SCORE = -1e30

_NT = (((1,), (1,)), ((), ()))


def _lane_iota(shape):
    return lax.broadcasted_iota(jnp.int32, shape, len(shape) - 1)


def _rmsnorm_bf16(x, gain):
    ms = jnp.mean(x * x, axis=-1, keepdims=True)
    return (x * lax.rsqrt(ms + EPS) * gain).astype(jnp.bfloat16)


def _silu(g):
    return g * (1.0 / (1.0 + jnp.exp(-g)))


def _softplus_log2(z2):
    return jnp.maximum(z2, 0.0) + jnp.log2(1.0 + jnp.exp2(-jnp.abs(z2)))


def _pair_masks():
    row2 = lax.broadcasted_iota(jnp.int32, (2 * BLOCK, LANES), 0)
    lane2 = lax.broadcasted_iota(jnp.int32, (2 * BLOCK, LANES), 1)
    own_head = (row2 < BLOCK) == (lane2 < HEAD_DIM)
    low = _lane_iota((1, LANES)) < HEAD_DIM

    def stack_heads(q_pair):
        q2 = jnp.concatenate([q_pair, q_pair], axis=0)
        return jnp.where(own_head, q2, jnp.zeros_like(q2))

    def unstack_heads(o2):
        return jnp.where(low, o2[:BLOCK], o2[BLOCK:])

    return row2, lane2, stack_heads, unstack_heads


def _inproj_kernel(
    sinks_ref, x_ref, pos_ref, gain_ref, w_f32_ref, qgain_ref, kgain_ref, invf_ref,
    o_ref, w_ref, h_ref, qa_ref, ga_ref, kv_ref, vt_ref, *, tiles_per_seq,
):
    tm = x_ref.shape[0]
    first_tile = (pl.program_id(0) % tiles_per_seq) == 0

    @pl.when(pl.program_id(0) == 0)
    def _():
        w_ref[...] = w_f32_ref[...].astype(w_ref.dtype)

    @pl.when(first_tile)
    def _():
        kv_ref[0:BLOCK, :] = jnp.zeros((BLOCK, kv_ref.shape[1]), kv_ref.dtype)
        vt_ref[:, :, 0:BLOCK] = jnp.zeros((SWA_KV_HEADS, LANES, BLOCK), vt_ref.dtype)

    h_ref[...] = _rmsnorm_bf16(x_ref[...], gain_ref[...])

    lane = _lane_iota((1, LANES))
    first_half = (lane % HEAD_DIM) < (HEAD_DIM // 2)
    low = lane < HEAD_DIM
    n_groups = LANES // (HEAD_DIM // 2)
    group = lane // (HEAD_DIM // 2)
    packed_ang = pos_ref[...].astype(jnp.float32) * invf_ref[...]

    def spread(packed):
        rolled = [packed] + [pltpu.roll(packed, s * (HEAD_DIM // 2), 1) for s in range(1, n_groups)]
        quarters = []
        for g in range(n_groups):
            t = rolled[(n_groups - 1 - g) % n_groups]
            for j in reversed(range(n_groups - 1)):
                t = jnp.where(group == j, rolled[(j - g) % n_groups], t)
            quarters.append(t)
        return jnp.concatenate(quarters, axis=0)

    cos = spread(jnp.cos(packed_ang))
    sin_signed = spread(jnp.sin(packed_ang)) * jnp.where(first_half, -1.0, 1.0)

    r = lax.broadcasted_iota(jnp.int32, (LANES, LANES), 0) // HEAD_DIM
    c = lax.broadcasted_iota(jnp.int32, (LANES, LANES), 1) // HEAD_DIM
    head_ones = jnp.where(r == c, 1.0, 0.0).astype(jnp.bfloat16)

    def norm_rope(a, head_gain, scale):
        ss = jnp.dot((a * a).astype(jnp.bfloat16), head_ones, preferred_element_type=jnp.float32)
        y = a * lax.rsqrt(ss * (1.0 / HEAD_DIM) + EPS) * head_gain
        partner = jnp.where(
            first_half, pltpu.roll(y, LANES - HEAD_DIM // 2, 1), pltpu.roll(y, HEAD_DIM // 2, 1)
        )
        y = y * cos + partner * sin_signed
        return y * scale if scale != 1.0 else y

    def project(src_col):
        return jnp.dot(h_ref[...], w_ref[:, src_col : src_col + CHUNK], preferred_element_type=jnp.float32)

    def store(slot, col, val):
        o_ref[:, slot * SLOT + col : slot * SLOT + col + val.shape[1]] = val.astype(o_ref.dtype)

    q_scale = LOG2E / math.sqrt(HEAD_DIM)

    def swa_queries(acc, col):
        for half in range(CHUNK // LANES):
            a = acc[:, half * LANES : (half + 1) * LANES]
            lanes = slice(col + half * LANES, col + (half + 1) * LANES)
            qa_ref[:, lanes] = norm_rope(a, qgain_ref[...], q_scale).astype(qa_ref.dtype)

    def swa_keys_values(acc):
        k = norm_rope(acc[:, :LANES], kgain_ref[...], 1.0)
        swapped = pltpu.roll(k, HEAD_DIM, 1)
        kv_ref[BLOCK:, :LANES] = jnp.where(low, k, swapped).astype(kv_ref.dtype)
        kv_ref[BLOCK:, LANES:] = jnp.where(low, swapped, k).astype(kv_ref.dtype)
        v_t = acc[:, LANES:].T
        for g in range(SWA_KV_HEADS):
            head_t = v_t[g * HEAD_DIM : (g + 1) * HEAD_DIM]
            vt_ref[g, :, BLOCK:] = jnp.concatenate([head_t, head_t], axis=0).astype(vt_ref.dtype)

    row2, lane2, stack_heads, unstack_heads = _pair_masks()
    in_cur_block = lane2 <= (row2 & (BLOCK - 1))
    kv_group = [(2 * p) // (SWA_Q_HEADS // SWA_KV_HEADS) for p in range(SWA_PAIRS)]

    def swa_scores(j):
        scores = []
        for p in range(SWA_PAIRS):
            q2 = stack_heads(qa_ref[j * BLOCK : (j + 1) * BLOCK, p * LANES : (p + 1) * LANES])
            k_win = kv_ref[j * BLOCK : (j + 2) * BLOCK, kv_group[p] * LANES : (kv_group[p] + 1) * LANES]
            scores.append(lax.dot_general(q2, k_win, _NT, preferred_element_type=jnp.float32))
        return scores

    def swa_softmax(j, scores):
        probs = []
        for p in range(SWA_PAIRS):
            s = jnp.where(in_cur_block, scores[p][:, BLOCK:], scores[p][:, :BLOCK])
            if j == 0:
                has_prev_cells = jnp.logical_not((jnp.zeros_like(lane2) + first_tile.astype(jnp.int32)) > 0)
                s = jnp.where(jnp.logical_or(in_cur_block, has_prev_cells), s, -jnp.inf)
            sink = jnp.where(
                row2[:, :1] < BLOCK, LOG2E * sinks_ref[0, 2 * p], LOG2E * sinks_ref[0, 2 * p + 1]
            )
            m = jnp.max(s, axis=-1, keepdims=True)
            e = jnp.exp2(s - m)
            denom = jnp.sum(e, axis=-1, keepdims=True) + jnp.exp2(sink - m)
            e_both = jnp.concatenate(
                [jnp.where(in_cur_block, 0.0, e), jnp.where(in_cur_block, e, 0.0)], axis=1
            ).astype(jnp.bfloat16)
            probs.append((e_both, denom))
        return probs

    def swa_values(j, probs):
        rows = slice(j * BLOCK, (j + 1) * BLOCK)
        for p in range(SWA_PAIRS):
            e_both, denom = probs[p]
            o_t = lax.dot_general(
                vt_ref[kv_group[p], :, j * BLOCK : (j + 2) * BLOCK], e_both, _NT, preferred_element_type=jnp.float32
            )
            o_pair = jnp.concatenate([o_t[:HEAD_DIM, :BLOCK], o_t[HEAD_DIM:, BLOCK:]], axis=0).T
            scale = jnp.where(low, 1.0 / denom[:BLOCK], 1.0 / denom[BLOCK:])
            gate = ga_ref[rows, p * LANES : (p + 1) * LANES].astype(jnp.float32)
            o_ref[rows, SLOT_YA * SLOT + p * LANES : SLOT_YA * SLOT + (p + 1) * LANES] = (
                o_pair * scale * gate
            ).astype(o_ref.dtype)

    first, second = 0, CHUNK
    qa_first = project(SRC_QA + first)
    kva = project(SRC_KA)
    qa_second = project(SRC_QA + second)
    ga_ref[:, first : first + CHUNK] = _silu(project(SRC_GA + first)).astype(ga_ref.dtype)
    swa_queries(qa_first, first)
    ga_ref[:, second : second + CHUNK] = _silu(project(SRC_GA + second)).astype(ga_ref.dtype)
    swa_keys_values(kva)
    half_rows = tm // 2
    scale_q = lambda v: v * q_scale
    plain_pieces = [
        (slot, src, col, r0, post)
        for slot, src, post in (
            (SLOT_QB, SRC_QB, scale_q), (SLOT_KB, SRC_KB, None), (SLOT_VB, SRC_VB, None), (SLOT_GB, SRC_GB, _silu)
        )
        for col in range(0, SLOT, CHUNK)
        for r0 in (0, half_rows)
    ]

    def plain_piece(slot, src, col, r0, post):
        val = jnp.dot(
            h_ref[r0 : r0 + half_rows], w_ref[:, src + col : src + col + CHUNK], preferred_element_type=jnp.float32
        )
        val = post(val) if post else val
        o_ref[r0 : r0 + half_rows, slot * SLOT + col : slot * SLOT + col + CHUNK] = val.astype(o_ref.dtype)

    swa_queries(qa_second, second)
    n_blk = tm // BLOCK
    n_slots = 2 * n_blk
    by_slot = [[] for _ in range(n_slots)]
    for k, piece in enumerate(plain_pieces):
        by_slot[k * n_slots // len(plain_pieces)].append(piece)

    scores = swa_scores(0)
    for j in range(n_blk):
        for piece in by_slot[2 * j]:
            plain_piece(*piece)
        probs = swa_softmax(j, scores)
        if j + 1 < n_blk:
            scores = swa_scores(j + 1)
        for piece in by_slot[2 * j + 1]:
            plain_piece(*piece)
        swa_values(j, probs)

    kv_ref[0:BLOCK, :] = kv_ref[tm : tm + BLOCK, :]
    vt_ref[:, :, 0:BLOCK] = vt_ref[:, :, tm : tm + BLOCK]


def _inproj(sinks, x2, pos2, gain, w, qgain, kgain, invf, tm, seq_len):
    n, d = x2.shape
    full = lambda i: (0, 0)
    return pl.pallas_call(
        functools.partial(_inproj_kernel, tiles_per_seq=seq_len // tm),
        out_shape=jax.ShapeDtypeStruct((n, ACT_WIDTH), jnp.bfloat16),
        grid=(n // tm,),
        in_specs=[
            pl.BlockSpec(memory_space=pltpu.SMEM),
            pl.BlockSpec((tm, d), lambda i: (i, 0)),
            pl.BlockSpec((tm // (LANES // (HEAD_DIM // 2)), LANES), lambda i: (i, 0)),
            pl.BlockSpec((1, d), full),
            pl.BlockSpec(w.shape, full, pipeline_mode=pl.Buffered(1)),
            pl.BlockSpec((1, LANES), full),
            pl.BlockSpec((1, LANES), full),
            pl.BlockSpec((1, LANES), full),
        ],
        out_specs=pl.BlockSpec((tm, ACT_WIDTH), lambda i: (i, 0)),
        scratch_shapes=[
            pltpu.VMEM(w.shape, jnp.bfloat16),
            pltpu.VMEM((tm, d), jnp.bfloat16),
            pltpu.VMEM((tm, SWA_WIDTH), jnp.bfloat16),
            pltpu.VMEM((tm, SWA_WIDTH), jnp.bfloat16),
            pltpu.VMEM((BLOCK + tm, SWA_KV_HEADS * LANES), jnp.bfloat16),
            pltpu.VMEM((SWA_KV_HEADS, LANES, BLOCK + tm), jnp.bfloat16),
        ],
        compiler_params=pltpu.CompilerParams(
            dimension_semantics=("arbitrary",), vmem_limit_bytes=VMEM_LIMIT_BYTES
        ),
        name="inproj",
    )(sinks, x2, pos2, gain, w, qgain, kgain, invf)


def _mixer_kernel(
    x_ref, ya_ref, qb_ref, gb_ref, kb_ref, vb_ref, wout_f32_ref,
    o_ref, wout_ref, yb_ref, acc_ref, fail_ref, lowest_ref, lowest_smem, *, tq, steps_per_seq,
):
    t = pl.program_id(0)
    last_step = pl.num_programs(0) - 1
    step = t % steps_per_seq
    n_sub = tq // BLOCK
    row2, lane2, stack_heads, unstack_heads = _pair_masks()
    qrow_w = lax.broadcasted_iota(jnp.int32, (2 * BLOCK, CHUNK), 0) & (BLOCK - 1)
    key_w = lax.broadcasted_iota(jnp.int32, (2 * BLOCK, CHUNK), 1)
    causal_bias = jnp.where(lane2 < (row2 & (BLOCK - 1)), 0.0, MASKED_SCORE)
    kr = lax.broadcasted_iota(jnp.int32, (CHUNK, CHUNK), 0)
    kc = lax.broadcasted_iota(jnp.int32, (CHUNK, CHUNK), 1)
    suffix_ones = jnp.where(kr >= kc, 1.0, 0.0).astype(jnp.bfloat16)
    sb_pairs = range(SB_PAIRS)

    def sb_scores(p, q2, ks, mask):
        k_t = kb_ref[0, pl.ds(ks, CHUNK), p * LANES : (p + 1) * LANES]
        z = lax.dot_general(q2, k_t, _NT, preferred_element_type=jnp.float32)
        return mask(z)

    def mask_where(valid):
        return lambda z: jnp.where(valid, z, MASKED_SCORE)

    def mask_second_block_causal(z):
        return jnp.concatenate([z[:, :BLOCK], z[:, BLOCK:] + causal_bias], axis=1)

    def sb_suffix(z):
        return jnp.dot(
            _softplus_log2(z).astype(jnp.bfloat16), suffix_ones, preferred_element_type=jnp.float32
        )

    def sb_accumulate(sb, p, z, upto, ks, first):
        arg = z - upto
        if not first:
            fail = fail_ref[sb, p]
            arg = arg - jnp.concatenate([fail] * (CHUNK // LANES), axis=1)
        w = jnp.exp2(arg).astype(jnp.bfloat16)
        v_t = vb_ref[0, pl.ds(ks, CHUNK), p * LANES : (p + 1) * LANES]
        pv = jnp.dot(w, v_t, preferred_element_type=jnp.float32)
        total = jnp.broadcast_to(upto[:, :1], (2 * BLOCK, LANES))
        if first:
            acc_ref[sb, p] = pv
            fail_ref[sb, p] = total
            return total
        acc_ref[sb, p] += pv
        fail_ref[sb, p] = fail + total
        return fail + total

    def block_index(sb):
        return step * n_sub + sb

    def sb_queries(r0):
        return [stack_heads(qb_ref[0, pl.ds(r0, BLOCK), p * LANES : (p + 1) * LANES]) for p in sb_pairs]

    def stage_scores(sb):
        blk = block_index(sb)
        kp = pl.multiple_of(jnp.maximum(blk - 1, 0) * BLOCK, BLOCK)
        if sb == 0:
            mask = mask_where(key_w < (qrow_w + jnp.where(blk > 0, BLOCK, 0)))
        else:
            mask = mask_second_block_causal
        zs = [sb_scores(p, q2, kp, mask) for p, q2 in enumerate(sb_queries(sb * BLOCK))]
        return dict(blk=blk, kp=kp, zs=zs)

    def stage_suffix(st):
        st["uptos"] = [sb_suffix(z) for z in st["zs"]]

    def stage_values(sb, st):
        lowest = None
        for p in sb_pairs:
            f = sb_accumulate(sb, p, st["zs"][p], st["uptos"][p], st["kp"], first=True)
            lowest = f if lowest is None else jnp.minimum(lowest, f)
        if sb < 2:
            lowest = jnp.where(st["blk"] >= 2, lowest, jnp.inf)
        lowest_ref[sb] = jnp.min(lowest.reshape(-1, SUBLANES, LANES), axis=0)

    def finish_chunk(piece):
        c, r0 = piece
        rows = slice(r0, r0 + tq // 2)
        o_ref[0, rows, c : c + CHUNK] = (
            x_ref[0, rows, c : c + CHUNK]
            + jnp.dot(ya_ref[0, rows], wout_ref[:SWA_WIDTH, c : c + CHUNK], preferred_element_type=jnp.float32)
            + jnp.dot(yb_ref[rows], wout_ref[SWA_WIDTH:, c : c + CHUNK], preferred_element_type=jnp.float32)
        )

    all_chunks = [(c, r0) for c in range(0, o_ref.shape[2], CHUNK) for r0 in (0, tq // 2)]

    def attend(out_chunks):
        def next_out_chunk():
            if out_chunks:
                finish_chunk(out_chunks.pop(0))

        stage = stage_scores(0)
        for sb in range(n_sub):
            next_out_chunk()
            stage_suffix(stage)
            following = stage_scores(sb + 1) if sb + 1 < n_sub else None
            stage_values(sb, stage)
            stage = following
        while out_chunks:
            next_out_chunk()

        pl.when(jnp.min(lowest_ref[...]) < SB_DONE_LOG2)(visit_earlier_tiles)

        for sb in range(n_sub):
            rows = slice(sb * BLOCK, (sb + 1) * BLOCK)
            for p in sb_pairs:
                gate = gb_ref[0, rows, p * LANES : (p + 1) * LANES].astype(jnp.float32)
                yb_ref[rows, p * LANES : (p + 1) * LANES] = (
                    unstack_heads(acc_ref[sb, p]) * gate
                ).astype(yb_ref.dtype)

    def visit_earlier_tiles():
        for sb in range(n_sub):
            lowest_smem[sb] = jnp.min(lowest_ref[sb])

        def earlier_tiles(sb, carry):
            blk = block_index(sb)
            q2s = sb_queries(pl.multiple_of(sb * BLOCK, BLOCK))
            n_tiles = blk // 2

            def cond(c):
                n, lowest = c
                return jnp.logical_and(n < n_tiles, lowest < SB_DONE_LOG2)

            def body(c):
                n, _ = c
                start = (blk - 1) * BLOCK - CHUNK * (n + 1)
                ks = pl.multiple_of(jnp.maximum(start, 0), BLOCK)
                tile_valid = key_w < (CHUNK + jnp.minimum(start, 0))
                zs = [sb_scores(p, q2s[p], ks, mask_where(tile_valid)) for p in sb_pairs]
                uptos = [sb_suffix(z) for z in zs]
                lowest = None
                for p in sb_pairs:
                    f = sb_accumulate(sb, p, zs[p], uptos[p], ks, first=False)
                    lowest = f if lowest is None else jnp.minimum(lowest, f)
                return n + 1, jnp.min(lowest)

            lax.while_loop(cond, body, (jnp.int32(0), lowest_smem[sb]))
            return carry

        lax.fori_loop(0, n_sub, earlier_tiles, 0)

    @pl.when(t == 0)
    def _():
        wout_ref[...] = wout_f32_ref[...].astype(wout_ref.dtype)
        yb_ref[...] = jnp.zeros(yb_ref.shape, yb_ref.dtype)

    @pl.when(t < last_step)
    def _():
        attend(list(all_chunks))

    @pl.when(t == last_step)
    def _():
        for c in all_chunks:
            finish_chunk(c)


def _mixer(x, act, wout, tq):
    b, s, d = x.shape
    steps_per_seq = s // tq
    n_tiles = b * steps_per_seq

    def attended(t, last):
        tile = jnp.minimum(t, n_tiles - 1)
        return tile // steps_per_seq, tile % steps_per_seq, last

    def finished(t, last):
        tile = jnp.maximum(t - 1, 0)
        return tile // steps_per_seq, tile % steps_per_seq, last

    now = lambda slot: pl.BlockSpec((1, tq, SLOT), lambda t, slot=slot: attended(t, slot))
    seq = lambda slot: pl.BlockSpec((1, s, SLOT), lambda t, slot=slot: (attended(t, slot)[0], 0, slot))
    return pl.pallas_call(
        functools.partial(_mixer_kernel, tq=tq, steps_per_seq=steps_per_seq),
        out_shape=jax.ShapeDtypeStruct((b, s, d), jnp.float32),
        grid=(n_tiles + 1,),
        in_specs=[
            pl.BlockSpec((1, tq, d), lambda t: finished(t, 0)),
            pl.BlockSpec((1, tq, SLOT), lambda t: finished(t, SLOT_YA)),
            now(SLOT_QB), now(SLOT_GB),
            seq(SLOT_KB), seq(SLOT_VB),
            pl.BlockSpec(wout.shape, lambda t: (0, 0), pipeline_mode=pl.Buffered(1)),
        ],
        out_specs=pl.BlockSpec((1, tq, d), lambda t: finished(t, 0)),
        scratch_shapes=[
            pltpu.VMEM(wout.shape, jnp.bfloat16),
            pltpu.VMEM((tq, SB_WIDTH), jnp.bfloat16),
            pltpu.VMEM((tq // BLOCK, SB_PAIRS, 2 * BLOCK, LANES), jnp.float32),
            pltpu.VMEM((tq // BLOCK, SB_PAIRS, 2 * BLOCK, LANES), jnp.float32),
            pltpu.VMEM((tq // BLOCK, SUBLANES, LANES), jnp.float32),
            pltpu.SMEM((tq // BLOCK,), jnp.float32),
        ],
        compiler_params=pltpu.CompilerParams(
            dimension_semantics=("arbitrary",), vmem_limit_bytes=VMEM_LIMIT_BYTES
        ),
        name="mixer",
    )(x, act, act, act, act, act, wout)


def kernel(x, positions, norm_gain, w_in, q_norm_gain, k_norm_gain, sinks, w_out):
    b, s, d = x.shape
    tile = 1024
    assert s % tile == 0 and s >= CHUNK
    depth = w_in.shape[0]
    half = HEAD_DIM // 2
    inv_freq = ROPE_THETA ** (-jnp.arange(half, dtype=jnp.float32) * 2.0 / HEAD_DIM)
    invf = jnp.tile(inv_freq, LANES // half).reshape(1, LANES)
    n_groups = LANES // half
    packed_pos = jnp.repeat(
        positions.reshape(-1, n_groups, tile // n_groups).transpose(0, 2, 1), half, axis=2
    ).reshape(-1, LANES)
    for l in range(depth):
        act = _inproj(
            sinks[l].reshape(1, SWA_Q_HEADS),
            x.reshape(b * s, d),
            packed_pos,
            norm_gain[l].reshape(1, d),
            w_in[l],
            jnp.tile(q_norm_gain[l], LANES // HEAD_DIM).reshape(1, LANES),
            jnp.tile(k_norm_gain[l], LANES // HEAD_DIM).reshape(1, LANES),
            invf,
            tm=tile,
            seq_len=s,
        )
        x = _mixer(x, act.reshape(b, s, ACT_WIDTH), w_out[l], tq=tile)
    return x
```

```python
import functools
import math

import jax
import jax.numpy as jnp
from jax import lax
from jax.experimental import pallas as pl
from jax.experimental.pallas import tpu as pltpu

HEAD_DIM = 64
SWA_Q_HEADS = 8
SWA_KV_HEADS = 2
SB_HEADS = 8
BLOCK = 128
ROPE_THETA = 10000.0
EPS = 1e-6
LANES = 128
SUBLANES = 8
CHUNK = 256
LOG2E = math.log2(math.e)

SWA_WIDTH = SWA_Q_HEADS * HEAD_DIM
SWA_KV_WIDTH = SWA_KV_HEADS * HEAD_DIM
SB_WIDTH = SB_HEADS * HEAD_DIM
MIX_WIDTH = SWA_WIDTH + SB_WIDTH
SWA_PAIRS = SWA_Q_HEADS // 2
SB_PAIRS = SB_HEADS // 2

SRC_QA = 0
SRC_KA = SRC_QA + SWA_WIDTH
SRC_VA = SRC_KA + SWA_KV_WIDTH
SRC_GA = SRC_VA + SWA_KV_WIDTH
SRC_QB = SRC_GA + SWA_WIDTH
SRC_KB = SRC_QB + SB_WIDTH
SRC_VB = SRC_KB + SB_WIDTH
SRC_GB = SRC_VB + SB_WIDTH

SLOT = 512
SLOT_YA, SLOT_QB, SLOT_KB, SLOT_VB, SLOT_GB = range(5)
ACT_WIDTH = 5 * SLOT

VMEM_LIMIT_BYTES = 56 * 1024 * 1024

SB_DONE_LOG2 = 128.0
MASKED_SCORE = -1e30

_NT = (((1,), (1,)), ((), ()))


def _lane_iota(shape):
    return lax.broadcasted_iota(jnp.int32, shape, len(shape) - 1)


def _rmsnorm_bf16(x, gain):
    ms = jnp.mean(x * x, axis=-1, keepdims=True)
    return (x * lax.rsqrt(ms + EPS) * gain).astype(jnp.bfloat16)


def _silu(g):
    return g * (1.0 / (1.0 + jnp.exp(-g)))


def _softplus_log2(z2):
    return jnp.maximum(z2, 0.0) + jnp.log2(1.0 + jnp.exp2(-jnp.abs(z2)))


def _pair_masks():
    row2 = lax.broadcasted_iota(jnp.int32, (2 * BLOCK, LANES), 0)
    lane2 = lax.broadcasted_iota(jnp.int32, (2 * BLOCK, LANES), 1)
    own_head = (row2 < BLOCK) == (lane2 < HEAD_DIM)
    low = _lane_iota((1, LANES)) < HEAD_DIM

    def stack_heads(q_pair):
        q2 = jnp.concatenate([q_pair, q_pair], axis=0)
        return jnp.where(own_head, q2, jnp.zeros_like(q2))

    def unstack_heads(o2):
        return jnp.where(low, o2[:BLOCK], o2[BLOCK:])

    return row2, lane2, stack_heads, unstack_heads


def _inproj_kernel(
    sinks_ref, x_ref, pos_ref, gain_ref, w_f32_ref, qgain_ref, kgain_ref, invf_ref,
    o_ref, w_ref, h_ref, qa_ref, ga_ref, kv_ref, vt_ref, *, tiles_per_seq,
):
    tm = x_ref.shape[0]
    first_tile = (pl.program_id(0) % tiles_per_seq) == 0

    @pl.when(pl.program_id(0) == 0)
    def _():
        w_ref[...] = w_f32_ref[...].astype(w_ref.dtype)

    @pl.when(first_tile)
    def _():
        kv_ref[0:BLOCK, :] = jnp.zeros((BLOCK, kv_ref.shape[1]), kv_ref.dtype)
        vt_ref[:, :, 0:BLOCK] = jnp.zeros((SWA_KV_HEADS, LANES, BLOCK), vt_ref.dtype)

    h_ref[...] = _rmsnorm_bf16(x_ref[...], gain_ref[...])

    lane = _lane_iota((1, LANES))
    first_half = (lane % HEAD_DIM) < (HEAD_DIM // 2)
    low = lane < HEAD_DIM
    n_groups = LANES // (HEAD_DIM // 2)
    group = lane // (HEAD_DIM // 2)
    packed_ang = pos_ref[...].astype(jnp.float32) * invf_ref[...]

    def spread(packed):
        rolled = [packed] + [pltpu.roll(packed, s * (HEAD_DIM // 2), 1) for s in range(1, n_groups)]
        quarters = []
        for g in range(n_groups):
            t = rolled[(n_groups - 1 - g) % n_groups]
            for j in reversed(range(n_groups - 1)):
                t = jnp.where(group == j, rolled[(j - g) % n_groups], t)
            quarters.append(t)
        return jnp.concatenate(quarters, axis=0)

    cos = spread(jnp.cos(packed_ang))
    sin_signed = spread(jnp.sin(packed_ang)) * jnp.where(first_half, -1.0, 1.0)

    r = lax.broadcasted_iota(jnp.int32, (LANES, LANES), 0) // HEAD_DIM
    c = lax.broadcasted_iota(jnp.int32, (LANES, LANES), 1) // HEAD_DIM
    head_ones = jnp.where(r == c, 1.0, 0.0).astype(jnp.bfloat16)

    def per_head(gain_ref):
        return jnp.concatenate([gain_ref[...]] * (LANES // HEAD_DIM), axis=1)

    def norm_rope(a, head_gain, scale):
        ss = jnp.dot((a * a).astype(jnp.bfloat16), head_ones, preferred_element_type=jnp.float32)
        y = a * lax.rsqrt(ss * (1.0 / HEAD_DIM) + EPS) * head_gain
        partner = jnp.where(
            first_half, pltpu.roll(y, LANES - HEAD_DIM // 2, 1), pltpu.roll(y, HEAD_DIM // 2, 1)
        )
        y = y * cos + partner * sin_signed
        return y * scale if scale != 1.0 else y

    def project(src_col):
        return jnp.dot(h_ref[...], w_ref[:, src_col : src_col + CHUNK], preferred_element_type=jnp.float32)

    def store(slot, col, val):
        o_ref[:, slot * SLOT + col : slot * SLOT + col + val.shape[1]] = val.astype(o_ref.dtype)

    q_scale = LOG2E / math.sqrt(HEAD_DIM)

    def swa_queries(acc, col):
        for half in range(CHUNK // LANES):
            a = acc[:, half * LANES : (half + 1) * LANES]
            lanes = slice(col + half * LANES, col + (half + 1) * LANES)
            qa_ref[:, lanes] = norm_rope(a, per_head(qgain_ref), q_scale).astype(qa_ref.dtype)

    def swa_keys_values(acc):
        k = norm_rope(acc[:, :LANES], per_head(kgain_ref), 1.0)
        swapped = pltpu.roll(k, HEAD_DIM, 1)
        kv_ref[BLOCK:, :LANES] = jnp.where(low, k, swapped).astype(kv_ref.dtype)
        kv_ref[BLOCK:, LANES:] = jnp.where(low, swapped, k).astype(kv_ref.dtype)
        v_t = acc[:, LANES:].T
        for g in range(SWA_KV_HEADS):
            head_t = v_t[g * HEAD_DIM : (g + 1) * HEAD_DIM]
            vt_ref[g, :, BLOCK:] = jnp.concatenate([head_t, head_t], axis=0).astype(vt_ref.dtype)

    row2, lane2, stack_heads, unstack_heads = _pair_masks()
    in_cur_block = lane2 <= (row2 & (BLOCK - 1))
    kv_group = [(2 * p) // (SWA_Q_HEADS // SWA_KV_HEADS) for p in range(SWA_PAIRS)]

    def swa_scores(j):
        scores = []
        for p in range(SWA_PAIRS):
            q2 = stack_heads(qa_ref[j * BLOCK : (j + 1) * BLOCK, p * LANES : (p + 1) * LANES])
            k_win = kv_ref[j * BLOCK : (j + 2) * BLOCK, kv_group[p] * LANES : (kv_group[p] + 1) * LANES]
            scores.append(lax.dot_general(q2, k_win, _NT, preferred_element_type=jnp.float32))
        return scores

    def swa_softmax(j, scores):
        probs = []
        for p in range(SWA_PAIRS):
            s = jnp.where(in_cur_block, scores[p][:, BLOCK:], scores[p][:, :BLOCK])
            if j == 0:
                has_prev_cells = jnp.logical_not((jnp.zeros_like(lane2) + first_tile.astype(jnp.int32)) > 0)
                s = jnp.where(jnp.logical_or(in_cur_block, has_prev_cells), s, -jnp.inf)
            sink = jnp.where(
                row2[:, :1] < BLOCK, LOG2E * sinks_ref[0, 2 * p], LOG2E * sinks_ref[0, 2 * p + 1]
            )
            m = jnp.max(s, axis=-1, keepdims=True)
            e = jnp.exp2(s - m)
            denom = jnp.sum(e, axis=-1, keepdims=True) + jnp.exp2(sink - m)
            e_both = jnp.concatenate(
                [jnp.where(in_cur_block, 0.0, e), jnp.where(in_cur_block, e, 0.0)], axis=1
            ).astype(jnp.bfloat16)
            probs.append((e_both, denom))
        return probs

    def swa_values(j, probs):
        rows = slice(j * BLOCK, (j + 1) * BLOCK)
        for p in range(SWA_PAIRS):
            e_both, denom = probs[p]
            o_t = lax.dot_general(
                vt_ref[kv_group[p], :, j * BLOCK : (j + 2) * BLOCK], e_both, _NT, preferred_element_type=jnp.float32
            )
            o_pair = jnp.concatenate([o_t[:HEAD_DIM, :BLOCK], o_t[HEAD_DIM:, BLOCK:]], axis=0).T
            scale = jnp.where(low, 1.0 / denom[:BLOCK], 1.0 / denom[BLOCK:])
            gate = ga_ref[rows, p * LANES : (p + 1) * LANES].astype(jnp.float32)
            o_ref[rows, SLOT_YA * SLOT + p * LANES : SLOT_YA * SLOT + (p + 1) * LANES] = (
                o_pair * scale * gate
            ).astype(o_ref.dtype)

    first, second = 0, CHUNK
    qa_first = project(SRC_QA + first)
    kva = project(SRC_KA)
    qa_second = project(SRC_QA + second)
    ga_ref[:, first : first + CHUNK] = _silu(project(SRC_GA + first)).astype(ga_ref.dtype)
    swa_queries(qa_first, first)
    ga_ref[:, second : second + CHUNK] = _silu(project(SRC_GA + second)).astype(ga_ref.dtype)
    swa_keys_values(kva)
    half_rows = tm // 2
    scale_q = lambda v: v * q_scale
    plain_pieces = [
        (slot, src, col, r0, post)
        for slot, src, post in (
            (SLOT_QB, SRC_QB, scale_q), (SLOT_KB, SRC_KB, None), (SLOT_VB, SRC_VB, None), (SLOT_GB, SRC_GB, _silu)
        )
        for col in range(0, SLOT, CHUNK)
        for r0 in (0, half_rows)
    ]

    def plain_piece(slot, src, col, r0, post):
        val = jnp.dot(
            h_ref[r0 : r0 + half_rows], w_ref[:, src + col : src + col + CHUNK], preferred_element_type=jnp.float32
        )
        val = post(val) if post else val
        o_ref[r0 : r0 + half_rows, slot * SLOT + col : slot * SLOT + col + CHUNK] = val.astype(o_ref.dtype)

    swa_queries(qa_second, second)
    n_blk = tm // BLOCK
    n_slots = 2 * n_blk
    by_slot = [[] for _ in range(n_slots)]
    for k, piece in enumerate(plain_pieces):
        by_slot[k * n_slots // len(plain_pieces)].append(piece)

    scores = swa_scores(0)
    for j in range(n_blk):
        for piece in by_slot[2 * j]:
            plain_piece(*piece)
        probs = swa_softmax(j, scores)
        if j + 1 < n_blk:
            scores = swa_scores(j + 1)
        for piece in by_slot[2 * j + 1]:
            plain_piece(*piece)
        swa_values(j, probs)

    kv_ref[0:BLOCK, :] = kv_ref[tm : tm + BLOCK, :]
    vt_ref[:, :, 0:BLOCK] = vt_ref[:, :, tm : tm + BLOCK]


def _inproj(sinks, x2, pos2, gain, w, qgain, kgain, invf, tm, seq_len):
    n, d = x2.shape
    full = lambda i: (0, 0)
    return pl.pallas_call(
        functools.partial(_inproj_kernel, tiles_per_seq=seq_len // tm),
        out_shape=jax.ShapeDtypeStruct((n, ACT_WIDTH), jnp.bfloat16),
        grid=(n // tm,),
        in_specs=[
            pl.BlockSpec(memory_space=pltpu.SMEM),
            pl.BlockSpec((tm, d), lambda i: (i, 0)),
            pl.BlockSpec((tm // (LANES // (HEAD_DIM // 2)), LANES), lambda i: (i, 0)),
            pl.BlockSpec((1, d), full),
            pl.BlockSpec(w.shape, full, pipeline_mode=pl.Buffered(1)),
            pl.BlockSpec((1, HEAD_DIM), full),
            pl.BlockSpec((1, HEAD_DIM), full),
            pl.BlockSpec((1, LANES), full),
        ],
        out_specs=pl.BlockSpec((tm, ACT_WIDTH), lambda i: (i, 0)),
        scratch_shapes=[
            pltpu.VMEM(w.shape, jnp.bfloat16),
            pltpu.VMEM((tm, d), jnp.bfloat16),
            pltpu.VMEM((tm, SWA_WIDTH), jnp.bfloat16),
            pltpu.VMEM((tm, SWA_WIDTH), jnp.bfloat16),
            pltpu.VMEM((BLOCK + tm, SWA_KV_HEADS * LANES), jnp.bfloat16),
            pltpu.VMEM((SWA_KV_HEADS, LANES, BLOCK + tm), jnp.bfloat16),
        ],
        compiler_params=pltpu.CompilerParams(
            dimension_semantics=("arbitrary",), vmem_limit_bytes=VMEM_LIMIT_BYTES
        ),
        name="inproj",
    )(sinks, x2, pos2, gain, w, qgain, kgain, invf)


def _mixer_kernel(
    x_ref, ya_ref, qb_ref, gb_ref, kb_ref, vb_ref, wout_f32_ref,
    o_ref, wout_ref, yb_ref, acc_ref, fail_ref, lowest_ref, lowest_smem, *, tq, steps_per_seq,
):
    t = pl.program_id(0)
    last_step = pl.num_programs(0) - 1
    step = t % steps_per_seq
    n_sub = tq // BLOCK
    row2, lane2, stack_heads, unstack_heads = _pair_masks()
    qrow_w = lax.broadcasted_iota(jnp.int32, (2 * BLOCK, CHUNK), 0) & (BLOCK - 1)
    key_w = lax.broadcasted_iota(jnp.int32, (2 * BLOCK, CHUNK), 1)
    causal_bias = jnp.where(lane2 < (row2 & (BLOCK - 1)), 0.0, MASKED_SCORE)
    kr = lax.broadcasted_iota(jnp.int32, (CHUNK, CHUNK), 0)
    kc = lax.broadcasted_iota(jnp.int32, (CHUNK, CHUNK), 1)
    suffix_ones = jnp.where(kr >= kc, 1.0, 0.0).astype(jnp.bfloat16)
    sb_pairs = range(SB_PAIRS)

    def sb_scores(p, q2, ks, mask):
        k_t = kb_ref[0, pl.ds(ks, CHUNK), p * LANES : (p + 1) * LANES]
        z = lax.dot_general(q2, k_t, _NT, preferred_element_type=jnp.float32)
        return mask(z)

    def mask_where(valid):
        return lambda z: jnp.where(valid, z, MASKED_SCORE)

    def mask_second_block_causal(z):
        return jnp.concatenate([z[:, :BLOCK], z[:, BLOCK:] + causal_bias], axis=1)

    def sb_suffix(z):
        return jnp.dot(
            _softplus_log2(z).astype(jnp.bfloat16), suffix_ones, preferred_element_type=jnp.float32
        )

    def sb_accumulate(sb, p, z, upto, ks, first):
        arg = z - upto
        if not first:
            fail = fail_ref[sb, p]
            arg = arg - jnp.concatenate([fail] * (CHUNK // LANES), axis=1)
        w = jnp.exp2(arg).astype(jnp.bfloat16)
        v_t = vb_ref[0, pl.ds(ks, CHUNK), p * LANES : (p + 1) * LANES]
        pv = jnp.dot(w, v_t, preferred_element_type=jnp.float32)
        total = jnp.broadcast_to(upto[:, :1], (2 * BLOCK, LANES))
        if first:
            acc_ref[sb, p] = pv
            fail_ref[sb, p] = total
            return total
        acc_ref[sb, p] += pv
        fail_ref[sb, p] = fail + total
        return fail + total

    def block_index(sb):
        return step * n_sub + sb

    def sb_queries(r0):
        return [stack_heads(qb_ref[0, pl.ds(r0, BLOCK), p * LANES : (p + 1) * LANES]) for p in sb_pairs]

    def stage_scores(sb):
        blk = block_index(sb)
        kp = pl.multiple_of(jnp.maximum(blk - 1, 0) * BLOCK, BLOCK)
        if sb == 0:
            mask = mask_where(key_w < (qrow_w + jnp.where(blk > 0, BLOCK, 0)))
        else:
            mask = mask_second_block_causal
        zs = [sb_scores(p, q2, kp, mask) for p, q2 in enumerate(sb_queries(sb * BLOCK))]
        return dict(blk=blk, kp=kp, zs=zs)

    def stage_suffix(st):
        st["uptos"] = [sb_suffix(z) for z in st["zs"]]

    def stage_values(sb, st):
        lowest = None
        for p in sb_pairs:
            f = sb_accumulate(sb, p, st["zs"][p], st["uptos"][p], st["kp"], first=True)
            lowest = f if lowest is None else jnp.minimum(lowest, f)
        if sb < 2:
            lowest = jnp.where(st["blk"] >= 2, lowest, jnp.inf)
        lowest_ref[sb] = jnp.min(lowest.reshape(-1, SUBLANES, LANES), axis=0)

    def finish_chunk(piece):
        c, r0 = piece
        rows = slice(r0, r0 + tq // 2)
        o_ref[0, rows, c : c + CHUNK] = (
            x_ref[0, rows, c : c + CHUNK]
            + jnp.dot(ya_ref[0, rows], wout_ref[:SWA_WIDTH, c : c + CHUNK], preferred_element_type=jnp.float32)
            + jnp.dot(yb_ref[rows], wout_ref[SWA_WIDTH:, c : c + CHUNK], preferred_element_type=jnp.float32)
        )

    all_chunks = [(c, r0) for c in range(0, o_ref.shape[2], CHUNK) for r0 in (0, tq // 2)]

    def attend(out_chunks):
        def next_out_chunk():
            if out_chunks:
                finish_chunk(out_chunks.pop(0))

        stage = stage_scores(0)
        for sb in range(n_sub):
            next_out_chunk()
            stage_suffix(stage)
            following = stage_scores(sb + 1) if sb + 1 < n_sub else None
            stage_values(sb, stage)
            stage = following
        while out_chunks:
            next_out_chunk()

        pl.when(jnp.min(lowest_ref[...]) < SB_DONE_LOG2)(visit_earlier_tiles)

        for sb in range(n_sub):
            rows = slice(sb * BLOCK, (sb + 1) * BLOCK)
            for p in sb_pairs:
                gate = gb_ref[0, rows, p * LANES : (p + 1) * LANES].astype(jnp.float32)
                yb_ref[rows, p * LANES : (p + 1) * LANES] = (
                    unstack_heads(acc_ref[sb, p]) * gate
                ).astype(yb_ref.dtype)

    def visit_earlier_tiles():
        for sb in range(n_sub):
            lowest_smem[sb] = jnp.min(lowest_ref[sb])

        def earlier_tiles(sb, carry):
            blk = block_index(sb)
            q2s = sb_queries(pl.multiple_of(sb * BLOCK, BLOCK))
            n_tiles = blk // 2

            def cond(c):
                n, lowest = c
                return jnp.logical_and(n < n_tiles, lowest < SB_DONE_LOG2)

            def body(c):
                n, _ = c
                start = (blk - 1) * BLOCK - CHUNK * (n + 1)
                ks = pl.multiple_of(jnp.maximum(start, 0), BLOCK)
                tile_valid = key_w < (CHUNK + jnp.minimum(start, 0))
                zs = [sb_scores(p, q2s[p], ks, mask_where(tile_valid)) for p in sb_pairs]
                uptos = [sb_suffix(z) for z in zs]
                lowest = None
                for p in sb_pairs:
                    f = sb_accumulate(sb, p, zs[p], uptos[p], ks, first=False)
                    lowest = f if lowest is None else jnp.minimum(lowest, f)
                return n + 1, jnp.min(lowest)

            lax.while_loop(cond, body, (jnp.int32(0), lowest_smem[sb]))
            return carry

        lax.fori_loop(0, n_sub, earlier_tiles, 0)

    @pl.when(t == 0)
    def _():
        wout_ref[...] = wout_f32_ref[...].astype(wout_ref.dtype)
        yb_ref[...] = jnp.zeros(yb_ref.shape, yb_ref.dtype)

    @pl.when(t < last_step)
    def _():
        attend(list(all_chunks))

    @pl.when(t == last_step)
    def _():
        for c in all_chunks:
            finish_chunk(c)


def _mixer(x, act, wout, tq):
    b, s, d = x.shape
    steps_per_seq = s // tq
    n_tiles = b * steps_per_seq

    def attended(t, last):
        tile = jnp.minimum(t, n_tiles - 1)
        return tile // steps_per_seq, tile % steps_per_seq, last

    def finished(t, last):
        tile = jnp.maximum(t - 1, 0)
        return tile // steps_per_seq, tile % steps_per_seq, last

    now = lambda slot: pl.BlockSpec((1, tq, SLOT), lambda t, slot=slot: attended(t, slot))
    seq = lambda slot: pl.BlockSpec((1, s, SLOT), lambda t, slot=slot: (attended(t, slot)[0], 0, slot))
    return pl.pallas_call(
        functools.partial(_mixer_kernel, tq=tq, steps_per_seq=steps_per_seq),
        out_shape=jax.ShapeDtypeStruct((b, s, d), jnp.float32),
        grid=(n_tiles + 1,),
        in_specs=[
            pl.BlockSpec((1, tq, d), lambda t: finished(t, 0)),
            pl.BlockSpec((1, tq, SLOT), lambda t: finished(t, SLOT_YA)),
            now(SLOT_QB), now(SLOT_GB),
            seq(SLOT_KB), seq(SLOT_VB),
            pl.BlockSpec(wout.shape, lambda t: (0, 0), pipeline_mode=pl.Buffered(1)),
        ],
        out_specs=pl.BlockSpec((1, tq, d), lambda t: finished(t, 0)),
        scratch_shapes=[
            pltpu.VMEM(wout.shape, jnp.bfloat16),
            pltpu.VMEM((tq, SB_WIDTH), jnp.bfloat16),
            pltpu.VMEM((tq // BLOCK, SB_PAIRS, 2 * BLOCK, LANES), jnp.float32),
            pltpu.VMEM((tq // BLOCK, SB_PAIRS, 2 * BLOCK, LANES), jnp.float32),
            pltpu.VMEM((tq // BLOCK, SUBLANES, LANES), jnp.float32),
            pltpu.SMEM((tq // BLOCK,), jnp.float32),
        ],
        compiler_params=pltpu.CompilerParams(
            dimension_semantics=("arbitrary",), vmem_limit_bytes=VMEM_LIMIT_BYTES
        ),
        name="mixer",
    )(x, act, act, act, act, act, wout)


def kernel(x, positions, norm_gain, w_in, q_norm_gain, k_norm_gain, sinks, w_out):
    b, s, d = x.shape
    tile = 1024
    assert s % tile == 0 and s >= CHUNK
    depth = w_in.shape[0]
    half = HEAD_DIM // 2
    inv_freq = ROPE_THETA ** (-jnp.arange(half, dtype=jnp.float32) * 2.0 / HEAD_DIM)
    invf = jnp.tile(inv_freq, LANES // half).reshape(1, LANES)
    n_groups = LANES // half
    quarter = tile // n_groups
    packed_pos = (
        jnp.broadcast_to(positions.reshape(-1, n_groups, 1, quarter), (b * s // tile, n_groups, half, quarter))
        .transpose(0, 3, 1, 2)
        .reshape(-1, LANES)
    )
    for l in range(depth):
        act = _inproj(
            sinks[l].reshape(1, SWA_Q_HEADS),
            x.reshape(b * s, d),
            packed_pos,
            norm_gain[l].reshape(1, d),
            w_in[l],
            q_norm_gain[l].reshape(1, HEAD_DIM),
            k_norm_gain[l].reshape(1, HEAD_DIM),
            invf,
            tm=tile,
            seq_len=s,
        )
        x = _mixer(x, act.reshape(b, s, ACT_WIDTH), w_out[l], tq=tile)
    return x
```

```python
import functools
import math

import jax
import jax.numpy as jnp
from jax import lax
from jax.experimental import pallas as pl
from jax.experimental.pallas import tpu as pltpu

HEAD_DIM = 64
SWA_Q_HEADS = 8
SWA_KV_HEADS = 2
SB_HEADS = 8
BLOCK = 128
ROPE_THETA = 10000.0
EPS = 1e-6
LANES = 128
SUBLANES = 8
CHUNK = 256
LOG2E = math.log2(math.e)

SWA_WIDTH = SWA_Q_HEADS * HEAD_DIM
SWA_KV_WIDTH = SWA_KV_HEADS * HEAD_DIM
SB_WIDTH = SB_HEADS * HEAD_DIM
MIX_WIDTH = SWA_WIDTH + SB_WIDTH
SWA_PAIRS = SWA_Q_HEADS // 2
SB_PAIRS = SB_HEADS // 2

SRC_QA = 0
SRC_KA = SRC_QA + SWA_WIDTH
SRC_VA = SRC_KA + SWA_KV_WIDTH
SRC_GA = SRC_VA + SWA_KV_WIDTH
SRC_QB = SRC_GA + SWA_WIDTH
SRC_KB = SRC_QB + SB_WIDTH
SRC_VB = SRC_KB + SB_WIDTH
SRC_GB = SRC_VB + SB_WIDTH

SLOT = 512
SLOT_YA, SLOT_QB, SLOT_KB, SLOT_VB, SLOT_GB = range(5)
ACT_WIDTH = 5 * SLOT

VMEM_LIMIT_BYTES = 56 * 1024 * 1024

SB_DONE_LOG2 = 128.0
MASKED_SCORE = -1e30

_NT = (((1,), (1,)), ((), ()))


def _lane_iota(shape):
    return lax.broadcasted_iota(jnp.int32, shape, len(shape) - 1)


def _rmsnorm_bf16(x, gain):
    ms = jnp.mean(x * x, axis=-1, keepdims=True)
    return (x * lax.rsqrt(ms + EPS) * gain).astype(jnp.bfloat16)


def _silu(g):
    return g * (1.0 / (1.0 + jnp.exp(-g)))


def _softplus_log2(z2):
    return jnp.maximum(z2, 0.0) + jnp.log2(1.0 + jnp.exp2(-jnp.abs(z2)))


def _pair_masks():
    row2 = lax.broadcasted_iota(jnp.int32, (2 * BLOCK, LANES), 0)
    lane2 = lax.broadcasted_iota(jnp.int32, (2 * BLOCK, LANES), 1)
    own_head = (row2 < BLOCK) == (lane2 < HEAD_DIM)
    low = _lane_iota((1, LANES)) < HEAD_DIM

    def stack_heads(q_pair):
        q2 = jnp.concatenate([q_pair, q_pair], axis=0)
        return jnp.where(own_head, q2, jnp.zeros_like(q2))

    def unstack_heads(o2):
        return jnp.where(low, o2[:BLOCK], o2[BLOCK:])

    return row2, lane2, stack_heads, unstack_heads


def _inproj_kernel(
    sinks_ref, x_ref, pos_ref, gain_ref, w_f32_ref, qgain_ref, kgain_ref, invf_ref,
    o_ref, w_ref, h_ref, qa_ref, ga_ref, kv_ref, vt_ref, *, tiles_per_seq,
):
    tm = x_ref.shape[0]
    first_tile = (pl.program_id(0) % tiles_per_seq) == 0

    @pl.when(pl.program_id(0) == 0)
    def _():
        w_ref[...] = w_f32_ref[...].astype(w_ref.dtype)

    @pl.when(first_tile)
    def _():
        kv_ref[0:BLOCK, :] = jnp.zeros((BLOCK, kv_ref.shape[1]), kv_ref.dtype)
        vt_ref[:, :, 0:BLOCK] = jnp.zeros((SWA_KV_HEADS, LANES, BLOCK), vt_ref.dtype)

    h_ref[...] = _rmsnorm_bf16(x_ref[...], gain_ref[...])

    lane = _lane_iota((1, LANES))
    first_half = (lane % HEAD_DIM) < (HEAD_DIM // 2)
    low = lane < HEAD_DIM
    n_groups = LANES // (HEAD_DIM // 2)
    group = lane // (HEAD_DIM // 2)
    pos = pos_ref[0].astype(jnp.float32)
    packed_pos = jnp.concatenate(
        [jnp.broadcast_to(pos[g : g + 1, :], (HEAD_DIM // 2, pos.shape[1])) for g in range(n_groups)], axis=0
    ).T
    packed_ang = packed_pos * invf_ref[...]

    def spread(packed):
        rolled = [packed] + [pltpu.roll(packed, s * (HEAD_DIM // 2), 1) for s in range(1, n_groups)]
        quarters = []
        for g in range(n_groups):
            t = rolled[(n_groups - 1 - g) % n_groups]
            for j in reversed(range(n_groups - 1)):
                t = jnp.where(group == j, rolled[(j - g) % n_groups], t)
            quarters.append(t)
        return jnp.concatenate(quarters, axis=0)

    cos = spread(jnp.cos(packed_ang))
    sin_signed = spread(jnp.sin(packed_ang)) * jnp.where(first_half, -1.0, 1.0)

    r = lax.broadcasted_iota(jnp.int32, (LANES, LANES), 0) // HEAD_DIM
    c = lax.broadcasted_iota(jnp.int32, (LANES, LANES), 1) // HEAD_DIM
    head_ones = jnp.where(r == c, 1.0, 0.0).astype(jnp.bfloat16)

    def per_head(gain_ref):
        return jnp.concatenate([gain_ref[...]] * (LANES // HEAD_DIM), axis=1)

    def norm_rope(a, head_gain, scale):
        ss = jnp.dot((a * a).astype(jnp.bfloat16), head_ones, preferred_element_type=jnp.float32)
        y = a * lax.rsqrt(ss * (1.0 / HEAD_DIM) + EPS) * head_gain
        partner = jnp.where(
            first_half, pltpu.roll(y, LANES - HEAD_DIM // 2, 1), pltpu.roll(y, HEAD_DIM // 2, 1)
        )
        y = y * cos + partner * sin_signed
        return y * scale if scale != 1.0 else y

    def project(src_col):
        return jnp.dot(h_ref[...], w_ref[:, src_col : src_col + CHUNK], preferred_element_type=jnp.float32)

    def store(slot, col, val):
        o_ref[:, slot * SLOT + col : slot * SLOT + col + val.shape[1]] = val.astype(o_ref.dtype)

    q_scale = LOG2E / math.sqrt(HEAD_DIM)

    def swa_queries(acc, col):
        for half in range(CHUNK // LANES):
            a = acc[:, half * LANES : (half + 1) * LANES]
            lanes = slice(col + half * LANES, col + (half + 1) * LANES)
            qa_ref[:, lanes] = norm_rope(a, per_head(qgain_ref), q_scale).astype(qa_ref.dtype)

    def swa_keys_values(acc):
        k = norm_rope(acc[:, :LANES], per_head(kgain_ref), 1.0)
        swapped = pltpu.roll(k, HEAD_DIM, 1)
        kv_ref[BLOCK:, :LANES] = jnp.where(low, k, swapped).astype(kv_ref.dtype)
        kv_ref[BLOCK:, LANES:] = jnp.where(low, swapped, k).astype(kv_ref.dtype)
        v_t = acc[:, LANES:].T
        for g in range(SWA_KV_HEADS):
            head_t = v_t[g * HEAD_DIM : (g + 1) * HEAD_DIM]
            vt_ref[g, :, BLOCK:] = jnp.concatenate([head_t, head_t], axis=0).astype(vt_ref.dtype)

    row2, lane2, stack_heads, unstack_heads = _pair_masks()
    in_cur_block = lane2 <= (row2 & (BLOCK - 1))
    kv_group = [(2 * p) // (SWA_Q_HEADS // SWA_KV_HEADS) for p in range(SWA_PAIRS)]

    def swa_scores(j):
        scores = []
        for p in range(SWA_PAIRS):
            q2 = stack_heads(qa_ref[j * BLOCK : (j + 1) * BLOCK, p * LANES : (p + 1) * LANES])
            k_win = kv_ref[j * BLOCK : (j + 2) * BLOCK, kv_group[p] * LANES : (kv_group[p] + 1) * LANES]
            scores.append(lax.dot_general(q2, k_win, _NT, preferred_element_type=jnp.float32))
        return scores

    def swa_softmax(j, scores):
        probs = []
        for p in range(SWA_PAIRS):
            s = jnp.where(in_cur_block, scores[p][:, BLOCK:], scores[p][:, :BLOCK])
            if j == 0:
                has_prev_cells = jnp.logical_not((jnp.zeros_like(lane2) + first_tile.astype(jnp.int32)) > 0)
                s = jnp.where(jnp.logical_or(in_cur_block, has_prev_cells), s, -jnp.inf)
            sink = jnp.where(
                row2[:, :1] < BLOCK, LOG2E * sinks_ref[0, 2 * p], LOG2E * sinks_ref[0, 2 * p + 1]
            )
            m = jnp.max(s, axis=-1, keepdims=True)
            e = jnp.exp2(s - m)
            denom = jnp.sum(e, axis=-1, keepdims=True) + jnp.exp2(sink - m)
            e_both = jnp.concatenate(
                [jnp.where(in_cur_block, 0.0, e), jnp.where(in_cur_block, e, 0.0)], axis=1
            ).astype(jnp.bfloat16)
            probs.append((e_both, denom))
        return probs

    def swa_values(j, probs):
        rows = slice(j * BLOCK, (j + 1) * BLOCK)
        for p in range(SWA_PAIRS):
            e_both, denom = probs[p]
            o_t = lax.dot_general(
                vt_ref[kv_group[p], :, j * BLOCK : (j + 2) * BLOCK], e_both, _NT, preferred_element_type=jnp.float32
            )
            o_pair = jnp.concatenate([o_t[:HEAD_DIM, :BLOCK], o_t[HEAD_DIM:, BLOCK:]], axis=0).T
            scale = jnp.where(low, 1.0 / denom[:BLOCK], 1.0 / denom[BLOCK:])
            gate = ga_ref[rows, p * LANES : (p + 1) * LANES].astype(jnp.float32)
            o_ref[rows, SLOT_YA * SLOT + p * LANES : SLOT_YA * SLOT + (p + 1) * LANES] = (
                o_pair * scale * gate
            ).astype(o_ref.dtype)

    first, second = 0, CHUNK
    qa_first = project(SRC_QA + first)
    kva = project(SRC_KA)
    qa_second = project(SRC_QA + second)
    ga_ref[:, first : first + CHUNK] = _silu(project(SRC_GA + first)).astype(ga_ref.dtype)
    swa_queries(qa_first, first)
    ga_ref[:, second : second + CHUNK] = _silu(project(SRC_GA + second)).astype(ga_ref.dtype)
    swa_keys_values(kva)
    half_rows = tm // 2
    scale_q = lambda v: v * q_scale
    plain_pieces = [
        (slot, src, col, r0, post)
        for slot, src, post in (
            (SLOT_QB, SRC_QB, scale_q), (SLOT_KB, SRC_KB, None), (SLOT_VB, SRC_VB, None), (SLOT_GB, SRC_GB, _silu)
        )
        for col in range(0, SLOT, CHUNK)
        for r0 in (0, half_rows)
    ]

    def plain_piece(slot, src, col, r0, post):
        val = jnp.dot(
            h_ref[r0 : r0 + half_rows], w_ref[:, src + col : src + col + CHUNK], preferred_element_type=jnp.float32
        )
        val = post(val) if post else val
        o_ref[r0 : r0 + half_rows, slot * SLOT + col : slot * SLOT + col + CHUNK] = val.astype(o_ref.dtype)

    swa_queries(qa_second, second)
    n_blk = tm // BLOCK
    n_slots = 2 * n_blk
    by_slot = [[] for _ in range(n_slots)]
    for k, piece in enumerate(plain_pieces):
        by_slot[k * n_slots // len(plain_pieces)].append(piece)

    scores = swa_scores(0)
    for j in range(n_blk):
        for piece in by_slot[2 * j]:
            plain_piece(*piece)
        probs = swa_softmax(j, scores)
        if j + 1 < n_blk:
            scores = swa_scores(j + 1)
        for piece in by_slot[2 * j + 1]:
            plain_piece(*piece)
        swa_values(j, probs)

    kv_ref[0:BLOCK, :] = kv_ref[tm : tm + BLOCK, :]
    vt_ref[:, :, 0:BLOCK] = vt_ref[:, :, tm : tm + BLOCK]


def _inproj(sinks, x2, pos3, gain, w, qgain, kgain, invf, tm, seq_len):
    n, d = x2.shape
    full = lambda i: (0, 0)
    return pl.pallas_call(
        functools.partial(_inproj_kernel, tiles_per_seq=seq_len // tm),
        out_shape=jax.ShapeDtypeStruct((n, ACT_WIDTH), jnp.bfloat16),
        grid=(n // tm,),
        in_specs=[
            pl.BlockSpec(memory_space=pltpu.SMEM),
            pl.BlockSpec((tm, d), lambda i: (i, 0)),
            pl.BlockSpec((1,) + pos3.shape[1:], lambda i: (i, 0, 0)),
            pl.BlockSpec((1, d), full),
            pl.BlockSpec(w.shape, full, pipeline_mode=pl.Buffered(1)),
            pl.BlockSpec((1, HEAD_DIM), full),
            pl.BlockSpec((1, HEAD_DIM), full),
            pl.BlockSpec((1, LANES), full),
        ],
        out_specs=pl.BlockSpec((tm, ACT_WIDTH), lambda i: (i, 0)),
        scratch_shapes=[
            pltpu.VMEM(w.shape, jnp.bfloat16),
            pltpu.VMEM((tm, d), jnp.bfloat16),
            pltpu.VMEM((tm, SWA_WIDTH), jnp.bfloat16),
            pltpu.VMEM((tm, SWA_WIDTH), jnp.bfloat16),
            pltpu.VMEM((BLOCK + tm, SWA_KV_HEADS * LANES), jnp.bfloat16),
            pltpu.VMEM((SWA_KV_HEADS, LANES, BLOCK + tm), jnp.bfloat16),
        ],
        compiler_params=pltpu.CompilerParams(
            dimension_semantics=("arbitrary",), vmem_limit_bytes=VMEM_LIMIT_BYTES
        ),
        name="inproj",
    )(sinks, x2, pos3, gain, w, qgain, kgain, invf)


def _mixer_kernel(
    x_ref, ya_ref, qb_ref, gb_ref, kb_ref, vb_ref, wout_f32_ref,
    o_ref, wout_ref, yb_ref, acc_ref, fail_ref, lowest_ref, lowest_smem, *, tq, steps_per_seq,
):
    t = pl.program_id(0)
    last_step = pl.num_programs(0) - 1
    step = t % steps_per_seq
    n_sub = tq // BLOCK
    row2, lane2, stack_heads, unstack_heads = _pair_masks()
    qrow_w = lax.broadcasted_iota(jnp.int32, (2 * BLOCK, CHUNK), 0) & (BLOCK - 1)
    key_w = lax.broadcasted_iota(jnp.int32, (2 * BLOCK, CHUNK), 1)
    causal_bias = jnp.where(lane2 < (row2 & (BLOCK - 1)), 0.0, MASKED_SCORE)
    kr = lax.broadcasted_iota(jnp.int32, (CHUNK, CHUNK), 0)
    kc = lax.broadcasted_iota(jnp.int32, (CHUNK, CHUNK), 1)
    suffix_ones = jnp.where(kr >= kc, 1.0, 0.0).astype(jnp.bfloat16)
    sb_pairs = range(SB_PAIRS)

    def sb_scores(p, q2, ks, mask):
        k_t = kb_ref[0, pl.ds(ks, CHUNK), p * LANES : (p + 1) * LANES]
        z = lax.dot_general(q2, k_t, _NT, preferred_element_type=jnp.float32)
        return mask(z)

    def mask_where(valid):
        return lambda z: jnp.where(valid, z, MASKED_SCORE)

    def mask_second_block_causal(z):
        return jnp.concatenate([z[:, :BLOCK], z[:, BLOCK:] + causal_bias], axis=1)

    def sb_suffix(z):
        return jnp.dot(
            _softplus_log2(z).astype(jnp.bfloat16), suffix_ones, preferred_element_type=jnp.float32
        )

    def sb_accumulate(sb, p, z, upto, ks, first):
        arg = z - upto
        if not first:
            fail = fail_ref[sb, p]
            arg = arg - jnp.concatenate([fail] * (CHUNK // LANES), axis=1)
        w = jnp.exp2(arg).astype(jnp.bfloat16)
        v_t = vb_ref[0, pl.ds(ks, CHUNK), p * LANES : (p + 1) * LANES]
        pv = jnp.dot(w, v_t, preferred_element_type=jnp.float32)
        total = jnp.broadcast_to(upto[:, :1], (2 * BLOCK, LANES))
        if first:
            acc_ref[sb, p] = pv
            fail_ref[sb, p] = total
            return total
        acc_ref[sb, p] += pv
        fail_ref[sb, p] = fail + total
        return fail + total

    def block_index(sb):
        return step * n_sub + sb

    def sb_queries(r0):
        return [stack_heads(qb_ref[0, pl.ds(r0, BLOCK), p * LANES : (p + 1) * LANES]) for p in sb_pairs]

    def stage_scores(sb):
        blk = block_index(sb)
        kp = pl.multiple_of(jnp.maximum(blk - 1, 0) * BLOCK, BLOCK)
        if sb == 0:
            mask = mask_where(key_w < (qrow_w + jnp.where(blk > 0, BLOCK, 0)))
        else:
            mask = mask_second_block_causal
        zs = [sb_scores(p, q2, kp, mask) for p, q2 in enumerate(sb_queries(sb * BLOCK))]
        return dict(blk=blk, kp=kp, zs=zs)

    def stage_suffix(st):
        st["uptos"] = [sb_suffix(z) for z in st["zs"]]

    def stage_values(sb, st):
        lowest = None
        for p in sb_pairs:
            f = sb_accumulate(sb, p, st["zs"][p], st["uptos"][p], st["kp"], first=True)
            lowest = f if lowest is None else jnp.minimum(lowest, f)
        if sb < 2:
            lowest = jnp.where(st["blk"] >= 2, lowest, jnp.inf)
        lowest_ref[sb] = jnp.min(lowest.reshape(-1, SUBLANES, LANES), axis=0)

    def finish_chunk(piece):
        c, r0 = piece
        rows = slice(r0, r0 + tq // 2)
        o_ref[0, rows, c : c + CHUNK] = (
            x_ref[0, rows, c : c + CHUNK]
            + jnp.dot(ya_ref[0, rows], wout_ref[:SWA_WIDTH, c : c + CHUNK], preferred_element_type=jnp.float32)
            + jnp.dot(yb_ref[rows], wout_ref[SWA_WIDTH:, c : c + CHUNK], preferred_element_type=jnp.float32)
        )

    all_chunks = [(c, r0) for c in range(0, o_ref.shape[2], CHUNK) for r0 in (0, tq // 2)]

    def attend(out_chunks):
        def next_out_chunk():
            if out_chunks:
                finish_chunk(out_chunks.pop(0))

        stage = stage_scores(0)
        for sb in range(n_sub):
            next_out_chunk()
            stage_suffix(stage)
            following = stage_scores(sb + 1) if sb + 1 < n_sub else None
            stage_values(sb, stage)
            stage = following
        while out_chunks:
            next_out_chunk()

        pl.when(jnp.min(lowest_ref[...]) < SB_DONE_LOG2)(visit_earlier_tiles)

        for sb in range(n_sub):
            rows = slice(sb * BLOCK, (sb + 1) * BLOCK)
            for p in sb_pairs:
                gate = gb_ref[0, rows, p * LANES : (p + 1) * LANES].astype(jnp.float32)
                yb_ref[rows, p * LANES : (p + 1) * LANES] = (
                    unstack_heads(acc_ref[sb, p]) * gate
                ).astype(yb_ref.dtype)

    def visit_earlier_tiles():
        for sb in range(n_sub):
            lowest_smem[sb] = jnp.min(lowest_ref[sb])

        def earlier_tiles(sb, carry):
            blk = block_index(sb)
            q2s = sb_queries(pl.multiple_of(sb * BLOCK, BLOCK))
            n_tiles = blk // 2

            def cond(c):
                n, lowest = c
                return jnp.logical_and(n < n_tiles, lowest < SB_DONE_LOG2)

            def body(c):
                n, _ = c
                start = (blk - 1) * BLOCK - CHUNK * (n + 1)
                ks = pl.multiple_of(jnp.maximum(start, 0), BLOCK)
                tile_valid = key_w < (CHUNK + jnp.minimum(start, 0))
                zs = [sb_scores(p, q2s[p], ks, mask_where(tile_valid)) for p in sb_pairs]
                uptos = [sb_suffix(z) for z in zs]
                lowest = None
                for p in sb_pairs:
                    f = sb_accumulate(sb, p, zs[p], uptos[p], ks, first=False)
                    lowest = f if lowest is None else jnp.minimum(lowest, f)
                return n + 1, jnp.min(lowest)

            lax.while_loop(cond, body, (jnp.int32(0), lowest_smem[sb]))
            return carry

        lax.fori_loop(0, n_sub, earlier_tiles, 0)

    @pl.when(t == 0)
    def _():
        wout_ref[...] = wout_f32_ref[...].astype(wout_ref.dtype)
        yb_ref[...] = jnp.zeros(yb_ref.shape, yb_ref.dtype)

    @pl.when(t < last_step)
    def _():
        attend(list(all_chunks))

    @pl.when(t == last_step)
    def _():
        for c in all_chunks:
            finish_chunk(c)


def _mixer(x, act, wout, tq):
    b, s, d = x.shape
    steps_per_seq = s // tq
    n_tiles = b * steps_per_seq

    def attended(t, last):
        tile = jnp.minimum(t, n_tiles - 1)
        return tile // steps_per_seq, tile % steps_per_seq, last

    def finished(t, last):
        tile = jnp.maximum(t - 1, 0)
        return tile // steps_per_seq, tile % steps_per_seq, last

    now = lambda slot: pl.BlockSpec((1, tq, SLOT), lambda t, slot=slot: attended(t, slot))
    seq = lambda slot: pl.BlockSpec((1, s, SLOT), lambda t, slot=slot: (attended(t, slot)[0], 0, slot))
    return pl.pallas_call(
        functools.partial(_mixer_kernel, tq=tq, steps_per_seq=steps_per_seq),
        out_shape=jax.ShapeDtypeStruct((b, s, d), jnp.float32),
        grid=(n_tiles + 1,),
        in_specs=[
            pl.BlockSpec((1, tq, d), lambda t: finished(t, 0)),
            pl.BlockSpec((1, tq, SLOT), lambda t: finished(t, SLOT_YA)),
            now(SLOT_QB), now(SLOT_GB),
            seq(SLOT_KB), seq(SLOT_VB),
            pl.BlockSpec(wout.shape, lambda t: (0, 0), pipeline_mode=pl.Buffered(1)),
        ],
        out_specs=pl.BlockSpec((1, tq, d), lambda t: finished(t, 0)),
        scratch_shapes=[
            pltpu.VMEM(wout.shape, jnp.bfloat16),
            pltpu.VMEM((tq, SB_WIDTH), jnp.bfloat16),
            pltpu.VMEM((tq // BLOCK, SB_PAIRS, 2 * BLOCK, LANES), jnp.float32),
            pltpu.VMEM((tq // BLOCK, SB_PAIRS, 2 * BLOCK, LANES), jnp.float32),
            pltpu.VMEM((tq // BLOCK, SUBLANES, LANES), jnp.float32),
            pltpu.SMEM((tq // BLOCK,), jnp.float32),
        ],
        compiler_params=pltpu.CompilerParams(
            dimension_semantics=("arbitrary",), vmem_limit_bytes=VMEM_LIMIT_BYTES
        ),
        name="mixer",
    )(x, act, act, act, act, act, wout)


def kernel(x, positions, norm_gain, w_in, q_norm_gain, k_norm_gain, sinks, w_out):
    b, s, d = x.shape
    tile = 1024
    assert s % tile == 0 and s >= CHUNK
    depth = w_in.shape[0]
    half = HEAD_DIM // 2
    inv_freq = ROPE_THETA ** (-jnp.arange(half, dtype=jnp.float32) * 2.0 / HEAD_DIM)
    invf = jnp.tile(inv_freq, LANES // half).reshape(1, LANES)
    n_groups = LANES // half
    pos3 = positions.reshape(-1, n_groups, tile // n_groups)
    for l in range(depth):
        act = _inproj(
            sinks[l].reshape(1, SWA_Q_HEADS),
            x.reshape(b * s, d),
            pos3,
            norm_gain[l].reshape(1, d),
            w_in[l],
            q_norm_gain[l].reshape(1, HEAD_DIM),
            k_norm_gain[l].reshape(1, HEAD_DIM),
            invf,
            tm=tile,
            seq_len=s,
        )
        x = _mixer(x, act.reshape(b, s, ACT_WIDTH), w_out[l], tq=tile)
    return x
```

```python
import functools
import math

import jax
import jax.numpy as jnp
from jax import lax
from jax.experimental import pallas as pl
from jax.experimental.pallas import tpu as pltpu

HEAD_DIM = 64
SWA_Q_HEADS = 8
SWA_KV_HEADS = 2
SB_HEADS = 8
BLOCK = 128
ROPE_THETA = 10000.0
EPS = 1e-6
LANES = 128
SUBLANES = 8
CHUNK = 256
LOG2E = math.log2(math.e)

SWA_WIDTH = SWA_Q_HEADS * HEAD_DIM
SWA_KV_WIDTH = SWA_KV_HEADS * HEAD_DIM
SB_WIDTH = SB_HEADS * HEAD_DIM
MIX_WIDTH = SWA_WIDTH + SB_WIDTH
SWA_PAIRS = SWA_Q_HEADS // 2
SB_PAIRS = SB_HEADS // 2
SB_ROWS = 64

SRC_QA = 0
SRC_KA = SRC_QA + SWA_WIDTH
SRC_VA = SRC_KA + SWA_KV_WIDTH
SRC_GA = SRC_VA + SWA_KV_WIDTH
SRC_QB = SRC_GA + SWA_WIDTH
SRC_KB = SRC_QB + SB_WIDTH
SRC_VB = SRC_KB + SB_WIDTH
SRC_GB = SRC_VB + SB_WIDTH

SLOT = 512
SLOT_YA, SLOT_QB, SLOT_KB, SLOT_VB, SLOT_GB = range(5)
ACT_WIDTH = 5 * SLOT

VMEM_LIMIT_BYTES = 56 * 1024 * 1024

SB_DONE_LOG2 = 128.0
MASKED_SCORE = -1e30

_NT = (((1,), (1,)), ((), ()))


def _lane_iota(shape):
    return lax.broadcasted_iota(jnp.int32, shape, len(shape) - 1)


def _rmsnorm_bf16(x, gain):
    ms = jnp.mean(x * x, axis=-1, keepdims=True)
    return (x * lax.rsqrt(ms + EPS) * gain).astype(jnp.bfloat16)


def _silu(g):
    return g * (1.0 / (1.0 + jnp.exp(-g)))


def _softplus_log2(z2):
    return jnp.maximum(z2, 0.0) + jnp.log2(1.0 + jnp.exp2(-jnp.abs(z2)))


def _pair_masks(block):
    row2 = lax.broadcasted_iota(jnp.int32, (2 * block, LANES), 0)
    lane2 = lax.broadcasted_iota(jnp.int32, (2 * block, LANES), 1)
    own_head = (row2 < block) == (lane2 < HEAD_DIM)
    low = _lane_iota((1, LANES)) < HEAD_DIM

    def stack_heads(q_pair):
        q2 = jnp.concatenate([q_pair, q_pair], axis=0)
        return jnp.where(own_head, q2, jnp.zeros_like(q2))

    def unstack_heads(o2):
        return jnp.where(low, o2[:block], o2[block:])

    return row2, lane2, stack_heads, unstack_heads


def _inproj_kernel(
    sinks_ref, x_ref, pos_ref, gain_ref, w_f32_ref, qgain_ref, kgain_ref, invf_ref,
    o_ref, w_ref, h_ref, qa_ref, ga_ref, kv_ref, vt_ref, *, tiles_per_seq,
):
    tm = x_ref.shape[0]
    first_tile = (pl.program_id(0) % tiles_per_seq) == 0

    @pl.when(pl.program_id(0) == 0)
    def _():
        w_ref[...] = w_f32_ref[...].astype(w_ref.dtype)

    @pl.when(first_tile)
    def _():
        kv_ref[0:BLOCK, :] = jnp.zeros((BLOCK, kv_ref.shape[1]), kv_ref.dtype)
        vt_ref[:, :, 0:BLOCK] = jnp.zeros((SWA_KV_HEADS, LANES, BLOCK), vt_ref.dtype)

    h_ref[...] = _rmsnorm_bf16(x_ref[...], gain_ref[...])

    lane = _lane_iota((1, LANES))
    first_half = (lane % HEAD_DIM) < (HEAD_DIM // 2)
    low = lane < HEAD_DIM
    n_groups = LANES // (HEAD_DIM // 2)
    group = lane // (HEAD_DIM // 2)
    pos = pos_ref[0].astype(jnp.float32)
    packed_pos = jnp.concatenate(
        [jnp.broadcast_to(pos[g : g + 1, :], (HEAD_DIM // 2, pos.shape[1])) for g in range(n_groups)], axis=0
    ).T
    packed_ang = packed_pos * invf_ref[...]

    def spread(packed):
        rolled = [packed] + [pltpu.roll(packed, s * (HEAD_DIM // 2), 1) for s in range(1, n_groups)]
        quarters = []
        for g in range(n_groups):
            t = rolled[(n_groups - 1 - g) % n_groups]
            for j in reversed(range(n_groups - 1)):
                t = jnp.where(group == j, rolled[(j - g) % n_groups], t)
            quarters.append(t)
        return jnp.concatenate(quarters, axis=0)

    cos = spread(jnp.cos(packed_ang))
    sin_signed = spread(jnp.sin(packed_ang)) * jnp.where(first_half, -1.0, 1.0)

    r = lax.broadcasted_iota(jnp.int32, (LANES, LANES), 0) // HEAD_DIM
    c = lax.broadcasted_iota(jnp.int32, (LANES, LANES), 1) // HEAD_DIM
    head_ones = jnp.where(r == c, 1.0, 0.0).astype(jnp.bfloat16)

    def per_head(gain_ref):
        return jnp.concatenate([gain_ref[...]] * (LANES // HEAD_DIM), axis=1)

    def norm_rope(a, head_gain, scale):
        ss = jnp.dot((a * a).astype(jnp.bfloat16), head_ones, preferred_element_type=jnp.float32)
        y = a * lax.rsqrt(ss * (1.0 / HEAD_DIM) + EPS) * head_gain
        partner = jnp.where(
            first_half, pltpu.roll(y, LANES - HEAD_DIM // 2, 1), pltpu.roll(y, HEAD_DIM // 2, 1)
        )
        y = y * cos + partner * sin_signed
        return y * scale if scale != 1.0 else y

    def project(src_col):
        return jnp.dot(h_ref[...], w_ref[:, src_col : src_col + CHUNK], preferred_element_type=jnp.float32)

    def store(slot, col, val):
        o_ref[:, slot * SLOT + col : slot * SLOT + col + val.shape[1]] = val.astype(o_ref.dtype)

    q_scale = LOG2E / math.sqrt(HEAD_DIM)

    def swa_queries(acc, col):
        for half in range(CHUNK // LANES):
            a = acc[:, half * LANES : (half + 1) * LANES]
            lanes = slice(col + half * LANES, col + (half + 1) * LANES)
            qa_ref[:, lanes] = norm_rope(a, per_head(qgain_ref), q_scale).astype(qa_ref.dtype)

    def swa_keys_values(acc):
        k = norm_rope(acc[:, :LANES], per_head(kgain_ref), 1.0)
        swapped = pltpu.roll(k, HEAD_DIM, 1)
        kv_ref[BLOCK:, :LANES] = jnp.where(low, k, swapped).astype(kv_ref.dtype)
        kv_ref[BLOCK:, LANES:] = jnp.where(low, swapped, k).astype(kv_ref.dtype)
        v_t = acc[:, LANES:].T
        for g in range(SWA_KV_HEADS):
            head_t = v_t[g * HEAD_DIM : (g + 1) * HEAD_DIM]
            vt_ref[g, :, BLOCK:] = jnp.concatenate([head_t, head_t], axis=0).astype(vt_ref.dtype)

    row2, lane2, stack_heads, unstack_heads = _pair_masks(BLOCK)
    in_cur_block = lane2 <= (row2 & (BLOCK - 1))
    kv_group = [(2 * p) // (SWA_Q_HEADS // SWA_KV_HEADS) for p in range(SWA_PAIRS)]

    def swa_scores(j):
        scores = []
        for p in range(SWA_PAIRS):
            q2 = stack_heads(qa_ref[j * BLOCK : (j + 1) * BLOCK, p * LANES : (p + 1) * LANES])
            k_win = kv_ref[j * BLOCK : (j + 2) * BLOCK, kv_group[p] * LANES : (kv_group[p] + 1) * LANES]
            scores.append(lax.dot_general(q2, k_win, _NT, preferred_element_type=jnp.float32))
        return scores

    def swa_softmax(j, scores):
        probs = []
        for p in range(SWA_PAIRS):
            s = jnp.where(in_cur_block, scores[p][:, BLOCK:], scores[p][:, :BLOCK])
            if j == 0:
                has_prev_cells = jnp.logical_not((jnp.zeros_like(lane2) + first_tile.astype(jnp.int32)) > 0)
                s = jnp.where(jnp.logical_or(in_cur_block, has_prev_cells), s, -jnp.inf)
            sink = jnp.where(
                row2[:, :1] < BLOCK, LOG2E * sinks_ref[0, 2 * p], LOG2E * sinks_ref[0, 2 * p + 1]
            )
            m = jnp.max(s, axis=-1, keepdims=True)
            e = jnp.exp2(s - m)
            denom = jnp.sum(e, axis=-1, keepdims=True) + jnp.exp2(sink - m)
            e_both = jnp.concatenate(
                [jnp.where(in_cur_block, 0.0, e), jnp.where(in_cur_block, e, 0.0)], axis=1
            ).astype(jnp.bfloat16)
            probs.append((e_both, denom))
        return probs

    def swa_values(j, probs):
        rows = slice(j * BLOCK, (j + 1) * BLOCK)
        for p in range(SWA_PAIRS):
            e_both, denom = probs[p]
            o_t = lax.dot_general(
                vt_ref[kv_group[p], :, j * BLOCK : (j + 2) * BLOCK], e_both, _NT, preferred_element_type=jnp.float32
            )
            o_pair = jnp.concatenate([o_t[:HEAD_DIM, :BLOCK], o_t[HEAD_DIM:, BLOCK:]], axis=0).T
            scale = jnp.where(low, 1.0 / denom[:BLOCK], 1.0 / denom[BLOCK:])
            gate = ga_ref[rows, p * LANES : (p + 1) * LANES].astype(jnp.float32)
            o_ref[rows, SLOT_YA * SLOT + p * LANES : SLOT_YA * SLOT + (p + 1) * LANES] = (
                o_pair * scale * gate
            ).astype(o_ref.dtype)

    first, second = 0, CHUNK
    qa_first = project(SRC_QA + first)
    kva = project(SRC_KA)
    qa_second = project(SRC_QA + second)
    ga_ref[:, first : first + CHUNK] = _silu(project(SRC_GA + first)).astype(ga_ref.dtype)
    swa_queries(qa_first, first)
    ga_ref[:, second : second + CHUNK] = _silu(project(SRC_GA + second)).astype(ga_ref.dtype)
    swa_keys_values(kva)
    half_rows = tm // 2
    scale_q = lambda v: v * q_scale
    plain_pieces = [
        (slot, src, col, r0, post)
        for slot, src, post in (
            (SLOT_QB, SRC_QB, scale_q), (SLOT_KB, SRC_KB, None), (SLOT_VB, SRC_VB, None), (SLOT_GB, SRC_GB, _silu)
        )
        for col in range(0, SLOT, CHUNK)
        for r0 in (0, half_rows)
    ]

    def plain_piece(slot, src, col, r0, post):
        val = jnp.dot(
            h_ref[r0 : r0 + half_rows], w_ref[:, src + col : src + col + CHUNK], preferred_element_type=jnp.float32
        )
        val = post(val) if post else val
        o_ref[r0 : r0 + half_rows, slot * SLOT + col : slot * SLOT + col + CHUNK] = val.astype(o_ref.dtype)

    swa_queries(qa_second, second)
    n_blk = tm // BLOCK
    n_slots = 2 * n_blk
    by_slot = [[] for _ in range(n_slots)]
    for k, piece in enumerate(plain_pieces):
        by_slot[k * n_slots // len(plain_pieces)].append(piece)

    scores = swa_scores(0)
    for j in range(n_blk):
        for piece in by_slot[2 * j]:
            plain_piece(*piece)
        probs = swa_softmax(j, scores)
        if j + 1 < n_blk:
            scores = swa_scores(j + 1)
        for piece in by_slot[2 * j + 1]:
            plain_piece(*piece)
        swa_values(j, probs)

    kv_ref[0:BLOCK, :] = kv_ref[tm : tm + BLOCK, :]
    vt_ref[:, :, 0:BLOCK] = vt_ref[:, :, tm : tm + BLOCK]


def _inproj(sinks, x2, pos3, gain, w, qgain, kgain, invf, tm, seq_len):
    n, d = x2.shape
    full = lambda i: (0, 0)
    return pl.pallas_call(
        functools.partial(_inproj_kernel, tiles_per_seq=seq_len // tm),
        out_shape=jax.ShapeDtypeStruct((n, ACT_WIDTH), jnp.bfloat16),
        grid=(n // tm,),
        in_specs=[
            pl.BlockSpec(memory_space=pltpu.SMEM),
            pl.BlockSpec((tm, d), lambda i: (i, 0)),
            pl.BlockSpec((1,) + pos3.shape[1:], lambda i: (i, 0, 0)),
            pl.BlockSpec((1, d), full),
            pl.BlockSpec(w.shape, full, pipeline_mode=pl.Buffered(1)),
            pl.BlockSpec((1, HEAD_DIM), full),
            pl.BlockSpec((1, HEAD_DIM), full),
            pl.BlockSpec((1, LANES), full),
        ],
        out_specs=pl.BlockSpec((tm, ACT_WIDTH), lambda i: (i, 0)),
        scratch_shapes=[
            pltpu.VMEM(w.shape, jnp.bfloat16),
            pltpu.VMEM((tm, d), jnp.bfloat16),
            pltpu.VMEM((tm, SWA_WIDTH), jnp.bfloat16),
            pltpu.VMEM((tm, SWA_WIDTH), jnp.bfloat16),
            pltpu.VMEM((BLOCK + tm, SWA_KV_HEADS * LANES), jnp.bfloat16),
            pltpu.VMEM((SWA_KV_HEADS, LANES, BLOCK + tm), jnp.bfloat16),
        ],
        compiler_params=pltpu.CompilerParams(
            dimension_semantics=("arbitrary",), vmem_limit_bytes=VMEM_LIMIT_BYTES
        ),
        name="inproj",
    )(sinks, x2, pos3, gain, w, qgain, kgain, invf)


def _mixer_kernel(
    x_ref, ya_ref, qb_ref, gb_ref, kb_ref, vb_ref, wout_f32_ref,
    o_ref, wout_ref, yb_ref, acc_ref, fail_ref, lowest_ref, lowest_smem, *, tq, steps_per_seq,
):
    t = pl.program_id(0)
    last_step = pl.num_programs(0) - 1
    step = t % steps_per_seq
    n_sub = tq // SB_ROWS
    row2, lane2, stack_heads, unstack_heads = _pair_masks(SB_ROWS)
    qrow_w = lax.broadcasted_iota(jnp.int32, (2 * SB_ROWS, CHUNK), 0) & (SB_ROWS - 1)
    key_w = lax.broadcasted_iota(jnp.int32, (2 * SB_ROWS, CHUNK), 1)
    causal_bias = jnp.where(lane2 < (row2 & (SB_ROWS - 1)) + (LANES - SB_ROWS), 0.0, MASKED_SCORE)
    kr = lax.broadcasted_iota(jnp.int32, (CHUNK, CHUNK), 0)
    kc = lax.broadcasted_iota(jnp.int32, (CHUNK, CHUNK), 1)
    suffix_ones = jnp.where(kr >= kc, 1.0, 0.0).astype(jnp.bfloat16)
    sb_pairs = range(SB_PAIRS)

    def sb_scores(p, q2, ks, mask):
        k_t = kb_ref[0, pl.ds(ks, CHUNK), p * LANES : (p + 1) * LANES]
        z = lax.dot_general(q2, k_t, _NT, preferred_element_type=jnp.float32)
        return mask(z)

    def mask_where(valid):
        return lambda z: jnp.where(valid, z, MASKED_SCORE)

    def mask_second_half_causal(z):
        return jnp.concatenate([z[:, :LANES], z[:, LANES:] + causal_bias], axis=1)

    def sb_suffix(z):
        return jnp.dot(
            _softplus_log2(z).astype(jnp.bfloat16), suffix_ones, preferred_element_type=jnp.float32
        )

    def sb_accumulate(sb, p, z, upto, ks, first):
        arg = z - upto
        if not first:
            fail = fail_ref[sb, p]
            arg = arg - jnp.concatenate([fail] * (CHUNK // LANES), axis=1)
        w = jnp.exp2(arg).astype(jnp.bfloat16)
        v_t = vb_ref[0, pl.ds(ks, CHUNK), p * LANES : (p + 1) * LANES]
        pv = jnp.dot(w, v_t, preferred_element_type=jnp.float32)
        total = jnp.broadcast_to(upto[:, :1], (2 * SB_ROWS, LANES))
        if first:
            acc_ref[sb, p] = pv
            fail_ref[sb, p] = total
            return total
        acc_ref[sb, p] += pv
        fail_ref[sb, p] = fail + total
        return fail + total

    lookback = CHUNK - SB_ROWS
    early = lookback // SB_ROWS

    def first_row(sb):
        return (step * n_sub + sb) * SB_ROWS

    def window_start(q0):
        return pl.multiple_of(jnp.maximum(q0 - lookback, 0), SB_ROWS)

    def sb_queries(r0):
        return [stack_heads(qb_ref[0, pl.ds(r0, SB_ROWS), p * LANES : (p + 1) * LANES]) for p in sb_pairs]

    def stage_scores(sb):
        q0 = first_row(sb)
        kp = window_start(q0)
        if sb < early:
            mask = mask_where(key_w < (qrow_w + (q0 - kp)))
        else:
            mask = mask_second_half_causal
        zs = [sb_scores(p, q2, kp, mask) for p, q2 in enumerate(sb_queries(sb * SB_ROWS))]
        return dict(kp=kp, zs=zs)

    def stage_suffix(st):
        st["uptos"] = [sb_suffix(z) for z in st["zs"]]

    def stage_values(sb, st):
        lowest = None
        for p in sb_pairs:
            f = sb_accumulate(sb, p, st["zs"][p], st["uptos"][p], st["kp"], first=True)
            lowest = f if lowest is None else jnp.minimum(lowest, f)
        if sb <= early:
            lowest = jnp.where(st["kp"] > 0, lowest, jnp.inf)
        lowest_ref[sb] = jnp.min(lowest.reshape(-1, SUBLANES, LANES), axis=0)

    def finish_chunk(piece):
        c, r0 = piece
        rows = slice(r0, r0 + tq // 2)
        o_ref[0, rows, c : c + CHUNK] = (
            x_ref[0, rows, c : c + CHUNK]
            + jnp.dot(ya_ref[0, rows], wout_ref[:SWA_WIDTH, c : c + CHUNK], preferred_element_type=jnp.float32)
            + jnp.dot(yb_ref[rows], wout_ref[SWA_WIDTH:, c : c + CHUNK], preferred_element_type=jnp.float32)
        )

    all_chunks = [(c, r0) for c in range(0, o_ref.shape[2], CHUNK) for r0 in (0, tq // 2)]

    def attend(out_chunks):
        def next_out_chunk():
            if out_chunks:
                finish_chunk(out_chunks.pop(0))

        stage = stage_scores(0)
        for sb in range(n_sub):
            if sb % 2 == 0:
                next_out_chunk()
            stage_suffix(stage)
            following = stage_scores(sb + 1) if sb + 1 < n_sub else None
            stage_values(sb, stage)
            stage = following
        while out_chunks:
            next_out_chunk()

        pl.when(jnp.min(lowest_ref[...]) < SB_DONE_LOG2)(visit_earlier_tiles)

        for sb in range(n_sub):
            rows = slice(sb * SB_ROWS, (sb + 1) * SB_ROWS)
            for p in sb_pairs:
                gate = gb_ref[0, rows, p * LANES : (p + 1) * LANES].astype(jnp.float32)
                yb_ref[rows, p * LANES : (p + 1) * LANES] = (
                    unstack_heads(acc_ref[sb, p]) * gate
                ).astype(yb_ref.dtype)

    def visit_earlier_tiles():
        for sb in range(n_sub):
            lowest_smem[sb] = jnp.min(lowest_ref[sb])

        def earlier_tiles(sb, carry):
            kp = window_start(first_row(sb))
            q2s = sb_queries(pl.multiple_of(sb * SB_ROWS, SB_ROWS))
            n_tiles = (kp + CHUNK - 1) // CHUNK

            def cond(c):
                n, lowest = c
                return jnp.logical_and(n < n_tiles, lowest < SB_DONE_LOG2)

            def body(c):
                n, _ = c
                start = kp - CHUNK * (n + 1)
                ks = pl.multiple_of(jnp.maximum(start, 0), SB_ROWS)
                tile_valid = key_w < (CHUNK + jnp.minimum(start, 0))
                zs = [sb_scores(p, q2s[p], ks, mask_where(tile_valid)) for p in sb_pairs]
                uptos = [sb_suffix(z) for z in zs]
                lowest = None
                for p in sb_pairs:
                    f = sb_accumulate(sb, p, zs[p], uptos[p], ks, first=False)
                    lowest = f if lowest is None else jnp.minimum(lowest, f)
                return n + 1, jnp.min(lowest)

            lax.while_loop(cond, body, (jnp.int32(0), lowest_smem[sb]))
            return carry

        lax.fori_loop(0, n_sub, earlier_tiles, 0)

    @pl.when(t == 0)
    def _():
        wout_ref[...] = wout_f32_ref[...].astype(wout_ref.dtype)
        yb_ref[...] = jnp.zeros(yb_ref.shape, yb_ref.dtype)

    @pl.when(t < last_step)
    def _():
        attend(list(all_chunks))

    @pl.when(t == last_step)
    def _():
        for c in all_chunks:
            finish_chunk(c)


def _mixer(x, act, wout, tq):
    b, s, d = x.shape
    steps_per_seq = s // tq
    n_tiles = b * steps_per_seq

    def attended(t, last):
        tile = jnp.minimum(t, n_tiles - 1)
        return tile // steps_per_seq, tile % steps_per_seq, last

    def finished(t, last):
        tile = jnp.maximum(t - 1, 0)
        return tile // steps_per_seq, tile % steps_per_seq, last

    now = lambda slot: pl.BlockSpec((1, tq, SLOT), lambda t, slot=slot: attended(t, slot))
    seq = lambda slot: pl.BlockSpec((1, s, SLOT), lambda t, slot=slot: (attended(t, slot)[0], 0, slot))
    return pl.pallas_call(
        functools.partial(_mixer_kernel, tq=tq, steps_per_seq=steps_per_seq),
        out_shape=jax.ShapeDtypeStruct((b, s, d), jnp.float32),
        grid=(n_tiles + 1,),
        in_specs=[
            pl.BlockSpec((1, tq, d), lambda t: finished(t, 0)),
            pl.BlockSpec((1, tq, SLOT), lambda t: finished(t, SLOT_YA)),
            now(SLOT_QB), now(SLOT_GB),
            seq(SLOT_KB), seq(SLOT_VB),
            pl.BlockSpec(wout.shape, lambda t: (0, 0), pipeline_mode=pl.Buffered(1)),
        ],
        out_specs=pl.BlockSpec((1, tq, d), lambda t: finished(t, 0)),
        scratch_shapes=[
            pltpu.VMEM(wout.shape, jnp.bfloat16),
            pltpu.VMEM((tq, SB_WIDTH), jnp.bfloat16),
            pltpu.VMEM((tq // SB_ROWS, SB_PAIRS, 2 * SB_ROWS, LANES), jnp.float32),
            pltpu.VMEM((tq // SB_ROWS, SB_PAIRS, 2 * SB_ROWS, LANES), jnp.float32),
            pltpu.VMEM((tq // SB_ROWS, SUBLANES, LANES), jnp.float32),
            pltpu.SMEM((tq // SB_ROWS,), jnp.float32),
        ],
        compiler_params=pltpu.CompilerParams(
            dimension_semantics=("arbitrary",), vmem_limit_bytes=VMEM_LIMIT_BYTES
        ),
        name="mixer",
    )(x, act, act, act, act, act, wout)


def kernel(x, positions, norm_gain, w_in, q_norm_gain, k_norm_gain, sinks, w_out):
    b, s, d = x.shape
    tile = 1024
    assert s % tile == 0 and s >= CHUNK
    depth = w_in.shape[0]
    half = HEAD_DIM // 2
    inv_freq = ROPE_THETA ** (-jnp.arange(half, dtype=jnp.float32) * 2.0 / HEAD_DIM)
    invf = jnp.tile(inv_freq, LANES // half).reshape(1, LANES)
    n_groups = LANES // half
    pos3 = positions.reshape(-1, n_groups, tile // n_groups)
    for l in range(depth):
        act = _inproj(
            sinks[l].reshape(1, SWA_Q_HEADS),
            x.reshape(b * s, d),
            pos3,
            norm_gain[l].reshape(1, d),
            w_in[l],
            q_norm_gain[l].reshape(1, HEAD_DIM),
            k_norm_gain[l].reshape(1, HEAD_DIM),
            invf,
            tm=tile,
            seq_len=s,
        )
        x = _mixer(x, act.reshape(b, s, ACT_WIDTH), w_out[l], tq=tile)
    return x
```
